```python
import math
import jax, jax.numpy as jnp
from jax import lax
import numpy as np

D_MODEL = 1024
BATCH = 16
SEQ = 4096
DEPTH = 2

MOBA_HEADS = 8
MOBA_HEAD_DIM = D_MODEL // 16
MOBA_WIDTH = MOBA_HEADS * MOBA_HEAD_DIM
MOBA_BLOCK = 256
MOBA_TOP_BLOCKS = 3
MOBA_QUERY_BLOCK = 128
ROPE_THETA = 500000.0
ROPE_DIM = MOBA_HEAD_DIM // 4
GLA_HEADS = 4
GLA_KEY_DIM = D_MODEL // 8
GLA_VAL_DIM = D_MODEL // 8
GLA_K_WIDTH = GLA_HEADS * GLA_KEY_DIM
GLA_V_WIDTH = GLA_HEADS * GLA_VAL_DIM
GLA_GATE_RANK = 16
GLA_GATE_TAU = 16.0
GLA_CHUNK = 64
N_BRANCHES = 2
IN_SPLITS = (MOBA_WIDTH, MOBA_WIDTH, MOBA_WIDTH, GLA_K_WIDTH, GLA_K_WIDTH, GLA_V_WIDTH, GLA_V_WIDTH, GLA_GATE_RANK, D_MODEL, D_MODEL)
IN_COLS = 3 * MOBA_WIDTH + 2 * GLA_K_WIDTH + 2 * GLA_V_WIDTH + GLA_GATE_RANK + N_BRANCHES * D_MODEL
FFN_DIM = 7 * D_MODEL // 2
N_EXPERTS = 8
TOP_K = 2
EXPERT_DIM = 7 * D_MODEL // 2
MOE_TOKEN_BLOCK = 1024
N_DENSE_LAYERS = (DEPTH + 1) // 2
N_MOE_LAYERS = DEPTH // 2
DEEPNORM_ALPHA = (2 * DEPTH) ** 0.25
DEEPNORM_BETA = (8 * DEPTH) ** -0.25
LN_EPS = 1e-5
MASK_VALUE = -1e30

kernel_name = "hybrid_moba_gla_deepnorm_moe"


def layer_norm(x, g, b):
    xf = x.astype(jnp.float32)
    mu = jnp.mean(xf, -1, keepdims=True)
    var = jnp.mean(jnp.square(xf - mu), -1, keepdims=True)
    return ((xf - mu) * lax.rsqrt(var + LN_EPS) * g.astype(jnp.float32) + b.astype(jnp.float32)).astype(x.dtype)


def rope_tables(seq):
    inv_freq = jnp.power(ROPE_THETA, -jnp.arange(0, ROPE_DIM, 2, dtype=jnp.float32) / ROPE_DIM)
    ang = jnp.arange(seq, dtype=jnp.float32)[:, None] * inv_freq[None, :]
    return jnp.cos(ang), jnp.sin(ang)


def partial_rotary(t, cos, sin):
    half = ROPE_DIM // 2
    tf = t[..., :ROPE_DIM].astype(jnp.float32)
    t1, t2 = tf[..., :half], tf[..., half:]
    c, s = cos[None, :, None, :], sin[None, :, None, :]
    rot = jnp.concatenate([t1 * c - t2 * s, t2 * c + t1 * s], -1).astype(t.dtype)
    return jnp.concatenate([rot, t[..., ROPE_DIM:]], -1)


def moba_attention(q, k, v):
    B, S, H, dh = q.shape
    nb = -(-S // MOBA_BLOCK)
    s_pad = nb * MOBA_BLOCK
    padw = ((0, 0), (0, s_pad - S), (0, 0), (0, 0))
    q, k, v = jnp.pad(q, padw), jnp.pad(k, padw), jnp.pad(v, padw)
    kb = k.reshape(B, nb, MOBA_BLOCK, H, dh)
    vb = v.reshape(B, nb, MOBA_BLOCK, H, dh)
    k_mean = jnp.mean(kb.astype(jnp.float32), axis=2)
    gate = jnp.einsum('bshd,bnhd->bhsn', q.astype(jnp.float32), k_mean)
    q_blk_id = jnp.arange(s_pad, dtype=jnp.int32) // MOBA_BLOCK
    past = jnp.arange(nb, dtype=jnp.int32)[None, :] < q_blk_id[:, None]
    gate = jnp.where(past[None, None], gate, -jnp.inf)
    k_sel = min(MOBA_TOP_BLOCKS, nb)
    _, sel = lax.top_k(gate, k_sel)
    valid = sel < q_blk_id[None, None, :, None]

    n_qc = s_pad // MOBA_QUERY_BLOCK
    qc = q.reshape(B, n_qc, MOBA_QUERY_BLOCK, H, dh).transpose(0, 1, 3, 2, 4).reshape(B * n_qc, H, MOBA_QUERY_BLOCK, dh)
    sel_c = sel.reshape(B, H, n_qc, MOBA_QUERY_BLOCK, k_sel).transpose(0, 2, 1, 3, 4).reshape(B * n_qc, H, MOBA_QUERY_BLOCK, k_sel)
    valid_c = valid.reshape(B, H, n_qc, MOBA_QUERY_BLOCK, k_sel).transpose(0, 2, 1, 3, 4).reshape(B * n_qc, H, MOBA_QUERY_BLOCK, k_sel)
    kbh = kb.transpose(0, 3, 1, 2, 4)
    vbh = vb.transpose(0, 3, 1, 2, 4)
    b_idx = jnp.repeat(jnp.arange(B, dtype=jnp.int32), n_qc)
    c_idx = jnp.tile(jnp.arange(n_qc, dtype=jnp.int32), B)
    h_idx = jnp.arange(H, dtype=jnp.int32)[:, None, None]
    scale = dh ** -0.5

    def query_block(args):
        b, c, q_b, s_idx, s_valid = args
        k_b, v_b = kbh[b], vbh[b]
        own = (c * MOBA_QUERY_BLOCK) // MOBA_BLOCK
        k_own = lax.dynamic_index_in_dim(k_b, own, axis=1, keepdims=False)
        v_own = lax.dynamic_index_in_dim(v_b, own, axis=1, keepdims=False)
        k_g = k_b[h_idx, s_idx]
        v_g = v_b[h_idx, s_idx]
        q_local = (c * MOBA_QUERY_BLOCK) % MOBA_BLOCK + jnp.arange(MOBA_QUERY_BLOCK, dtype=jnp.int32)
        causal = jnp.arange(MOBA_BLOCK, dtype=jnp.int32)[None, :] <= q_local[:, None]
        s_own = jnp.einsum('hqd,hkd->hqk', q_b, k_own).astype(jnp.float32) * scale
        s_own = jnp.where(causal[None], s_own, MASK_VALUE)
        s_g = jnp.einsum('hqd,hqnkd->hqnk', q_b, k_g).astype(jnp.float32) * scale
        s_g = jnp.where(s_valid[..., None], s_g, MASK_VALUE).reshape(H, MOBA_QUERY_BLOCK, k_sel * MOBA_BLOCK)
        p = jax.nn.softmax(jnp.concatenate([s_own, s_g], -1), axis=-1).astype(v.dtype)
        p_own = p[..., :MOBA_BLOCK]
        p_g = p[..., MOBA_BLOCK:].reshape(H, MOBA_QUERY_BLOCK, k_sel, MOBA_BLOCK)
        return jnp.einsum('hqk,hkd->hqd', p_own, v_own) + jnp.einsum('hqnk,hqnkd->hqd', p_g, v_g)

    out = lax.map(query_block, (b_idx, c_idx, qc, sel_c, valid_c))
    out = out.reshape(B, n_qc, H, MOBA_QUERY_BLOCK, dh).transpose(0, 1, 3, 2, 4).reshape(B, s_pad, H, dh)
    return out[:, :S]


def gla_attention(q, k, v, log_a):
    B, S, H, dk = q.shape
    dv = v.shape[-1]
    nc = S // GLA_CHUNK

    def chunks(t):
        return t.reshape(B, nc, GLA_CHUNK, H, t.shape[-1]).transpose(1, 0, 3, 2, 4)

    q, k, v, g = chunks(q), chunks(k), chunks(v), chunks(log_a)
    cum = jnp.cumsum(g, axis=3)
    cum_last = cum[:, :, :, -1:, :]
    q_dec = q * jnp.exp(cum)
    k_inv = k * jnp.exp(-cum)
    k_to_end = k * jnp.exp(cum_last - cum)
    causal = jnp.tril(jnp.ones((GLA_CHUNK, GLA_CHUNK), dtype=bool))
    att = jnp.where(causal, jnp.einsum('nbhcd,nbhkd->nbhck', q_dec, k_inv), 0.0)
    intra = jnp.einsum('nbhck,nbhkv->nbhcv', att, v)
    d_state = jnp.einsum('nbhcd,nbhcv->nbhdv', k_to_end, v)
    decay = jnp.exp(cum_last[:, :, :, 0, :])

    def step(state, xs):
        q_c, dec_c, ds_c = xs
        inter = jnp.einsum('bhcd,bhdv->bhcv', q_c, state)
        return dec_c[..., None] * state + ds_c, inter

    state0 = jnp.zeros((B, H, dk, dv), jnp.float32)
    _, inter = lax.scan(step, state0, (q_dec, decay, d_state))
    out = intra + inter
    return out.transpose(1, 0, 3, 2, 4).reshape(B, S, H, dv)


def head_norm(o, g):
    mu = jnp.mean(o, -1, keepdims=True)
    var = jnp.mean(jnp.square(o - mu), -1, keepdims=True)
    on = (o - mu) * lax.rsqrt(var + LN_EPS)
    B, S = o.shape[0], o.shape[1]
    return on.reshape(B, S, -1) * g.astype(jnp.float32)


def token_mixer(h, cos, sin, w_in, w_gla_gate_up, b_gla_gate, gla_norm_g, w_branch_a, w_branch_b, w_out):
    B, S, _ = h.shape
    proj = h @ w_in
    offsets = [int(o) for o in np.cumsum(IN_SPLITS)[:-1]]
    qa, ka, va, qg, kg, vg, rg, lg, gate_a, gate_b = jnp.split(proj, offsets, axis=-1)

    qa = partial_rotary(qa.reshape(B, S, MOBA_HEADS, MOBA_HEAD_DIM), cos, sin)
    ka = partial_rotary(ka.reshape(B, S, MOBA_HEADS, MOBA_HEAD_DIM), cos, sin)
    y_a = moba_attention(qa, ka, va.reshape(B, S, MOBA_HEADS, MOBA_HEAD_DIM)).reshape(B, S, MOBA_WIDTH)

    log_a = jax.nn.log_sigmoid((lg @ w_gla_gate_up + b_gla_gate).astype(jnp.float32)) / GLA_GATE_TAU
    o = gla_attention(
        qg.reshape(B, S, GLA_HEADS, GLA_KEY_DIM).astype(jnp.float32) * (GLA_KEY_DIM ** -0.5),
        kg.reshape(B, S, GLA_HEADS, GLA_KEY_DIM).astype(jnp.float32),
        vg.reshape(B, S, GLA_HEADS, GLA_VAL_DIM).astype(jnp.float32),
        log_a.reshape(B, S, GLA_HEADS, GLA_KEY_DIM))
    y_b = head_norm(o, gla_norm_g).astype(h.dtype) * jax.nn.silu(rg)

    merged = jax.nn.sigmoid(gate_a) * (y_a @ w_branch_a) + jax.nn.sigmoid(gate_b) * (y_b @ w_branch_b)
    return merged @ w_out


def swiglu(h, w_gate, w_up, w_down):
    return (jax.nn.silu(h @ w_gate) * (h @ w_up)) @ w_down


def moe_swiglu(h, w_router, w_gate, w_up, w_down):
    B, S, D = h.shape
    T = B * S
    t = h.reshape(T, D)
    logits = (t @ w_router).astype(jnp.float32)
    top_val, top_idx = lax.top_k(logits, TOP_K)
    top_w = jax.nn.softmax(top_val, axis=-1)
    combine = jnp.sum(jax.nn.one_hot(top_idx, N_EXPERTS, dtype=jnp.float32) * top_w[..., None], axis=1).astype(h.dtype)
    tb = math.gcd(T, MOE_TOKEN_BLOCK)
    nblk = T // tb

    def token_block(args):
        xb, cb = args
        hid = jax.nn.silu(jnp.einsum('td,edf->tef', xb, w_gate)) * jnp.einsum('td,edf->tef', xb, w_up)
        return jnp.einsum('tef,efd->td', hid * cb[:, :, None], w_down)

    y = lax.map(token_block, (t.reshape(nblk, tb, D), combine.reshape(nblk, tb, N_EXPERTS)))
    return y.reshape(B, S, D)


def setup_inputs(seed: int = 0) -> dict:
    key = jax.random.key(seed)
    ks = jax.random.split(key, 20)

    def nrm(k, shape, scale):
        return jax.random.normal(k, shape, jnp.float32) * scale

    return {
        "x": nrm(ks[0], (BATCH, SEQ, D_MODEL), 1.0),
        "w_in": nrm(ks[1], (DEPTH, D_MODEL, IN_COLS), D_MODEL ** -0.5),
        "w_gla_gate_up": nrm(ks[2], (DEPTH, GLA_GATE_RANK, GLA_K_WIDTH), GLA_GATE_RANK ** -0.5),
        "b_gla_gate": nrm(ks[3], (DEPTH, GLA_K_WIDTH), 0.1),
        "gla_norm_g": 1.0 + nrm(ks[4], (DEPTH, GLA_V_WIDTH), 0.02),
        "w_branch_a": nrm(ks[5], (DEPTH, MOBA_WIDTH, D_MODEL), MOBA_WIDTH ** -0.5 * DEEPNORM_BETA),
        "w_branch_b": nrm(ks[6], (DEPTH, GLA_V_WIDTH, D_MODEL), GLA_V_WIDTH ** -0.5 * DEEPNORM_BETA),
        "w_out": nrm(ks[7], (DEPTH, D_MODEL, D_MODEL), D_MODEL ** -0.5 * DEEPNORM_BETA),
        "ln_mix_g": 1.0 + nrm(ks[8], (DEPTH, D_MODEL), 0.02),
        "ln_mix_b": nrm(ks[9], (DEPTH, D_MODEL), 0.02),
        "ffn_w_gate": nrm(ks[10], (N_DENSE_LAYERS, D_MODEL, FFN_DIM), D_MODEL ** -0.5),
        "ffn_w_up": nrm(ks[11], (N_DENSE_LAYERS, D_MODEL, FFN_DIM), D_MODEL ** -0.5),
        "ffn_w_down": nrm(ks[12], (N_DENSE_LAYERS, FFN_DIM, D_MODEL), FFN_DIM ** -0.5 * DEEPNORM_BETA),
        "moe_w_router": nrm(ks[13], (N_MOE_LAYERS, D_MODEL, N_EXPERTS), D_MODEL ** -0.5),
        "moe_w_gate": nrm(ks[14], (N_MOE_LAYERS, N_EXPERTS, D_MODEL, EXPERT_DIM), D_MODEL ** -0.5),
        "moe_w_up": nrm(ks[15], (N_MOE_LAYERS, N_EXPERTS, D_MODEL, EXPERT_DIM), D_MODEL ** -0.5),
        "moe_w_down": nrm(ks[16], (N_MOE_LAYERS, N_EXPERTS, EXPERT_DIM, D_MODEL), EXPERT_DIM ** -0.5 * DEEPNORM_BETA),
        "ln_ffn_g": 1.0 + nrm(ks[17], (DEPTH, D_MODEL), 0.02),
        "ln_ffn_b": nrm(ks[18], (DEPTH, D_MODEL), 0.02),
    }


def reference(x, w_in, w_gla_gate_up, b_gla_gate, gla_norm_g, w_branch_a, w_branch_b, w_out,
              ln_mix_g, ln_mix_b, ffn_w_gate, ffn_w_up, ffn_w_down, moe_w_router, moe_w_gate,
              moe_w_up, moe_w_down, ln_ffn_g, ln_ffn_b):
    cos, sin = rope_tables(x.shape[1])
    for layer in range(DEPTH):
        mix = token_mixer(x, cos, sin, w_in[layer], w_gla_gate_up[layer], b_gla_gate[layer],
                          gla_norm_g[layer], w_branch_a[layer], w_branch_b[layer], w_out[layer])
        x = layer_norm(DEEPNORM_ALPHA * x + mix, ln_mix_g[layer], ln_mix_b[layer])
        j = layer // 2
        if layer % 2 == 0:
            f = swiglu(x, ffn_w_gate[j], ffn_w_up[j], ffn_w_down[j])
        else:
            f = moe_swiglu(x, moe_w_router[j], moe_w_gate[j], moe_w_up[j], moe_w_down[j])
        x = layer_norm(DEEPNORM_ALPHA * x + f, ln_ffn_g[layer], ln_ffn_b[layer])
    return x
```

```python
import functools
import math

import jax
import jax.numpy as jnp
import numpy as np
from jax import lax
from jax.experimental import pallas as pl
from jax.experimental.pallas import tpu as pltpu

D_MODEL = 1024
DEPTH = 2
MOBA_HEADS = 8
MOBA_HEAD_DIM = 64
MOBA_WIDTH = 512
MOBA_BLOCK = 256
MOBA_TOP_BLOCKS = 3
ROPE_THETA = 500000.0
ROPE_DIM = 16
GLA_HEADS = 4
GLA_KEY_DIM = 128
GLA_WIDTH = 512
GLA_GATE_RANK = 16
GLA_GATE_TAU = 16.0
GLA_CHUNK = 64
FFN_DIM = 3584
N_EXPERTS = 8
DEEPNORM_ALPHA = (2 * DEPTH) ** 0.25
LN_EPS = 1e-5
MASK_VALUE = -1e30

LANES = 128
VMEM_LIMIT = 56 * 1024 * 1024

COL_GATE_A, COL_GATE_B = 0, 1024
COL_QA, COL_KA, COL_VA = 2048, 2560, 3072
COL_QG, COL_KG, COL_VG, COL_RG = 3584, 4096, 4608, 5120
MAIN_COLS = 5632


def _params(sem):
    return pltpu.CompilerParams(dimension_semantics=sem, vmem_limit_bytes=VMEM_LIMIT)


def _matmul_kernel(x_ref, w_ref, o_ref):
    o_ref[...] = jnp.dot(x_ref[...], w_ref[...], preferred_element_type=jnp.float32).astype(o_ref.dtype)


def matmul(x, w, out_dtype, tm, tn):
    m, k = x.shape
    n = w.shape[1]
    return pl.pallas_call(
        _matmul_kernel,
        grid=(m // tm, n // tn),
        in_specs=[pl.BlockSpec((tm, k), lambda i, j: (i, 0)),
                  pl.BlockSpec((k, tn), lambda i, j: (0, j))],
        out_specs=pl.BlockSpec((tm, tn), lambda i, j: (i, j)),
        out_shape=jax.ShapeDtypeStruct((m, n), out_dtype),
        compiler_params=_params(("parallel", "arbitrary")),
        name="in_proj",
    )(x, w)


def _rope_kernel(q_ref, k_ref, v_ref, c_ref, a_ref, b_ref, qo_ref, ko_ref, vo_ref, km_ref):
    def rot(t):
        out = []
        for j in range(MOBA_WIDTH // LANES):
            tj = t[:, j * LANES:(j + 1) * LANES]
            out.append(tj * c_ref[...] + pltpu.roll(tj, LANES - ROPE_DIM // 2, 1) * a_ref[...]
                       + pltpu.roll(tj, ROPE_DIM // 2, 1) * b_ref[...])
        return jnp.concatenate(out, axis=1)

    qo_ref[...] = rot(q_ref[...]).astype(qo_ref.dtype)
    kr = rot(k_ref[...])
    ko_ref[...] = kr.astype(ko_ref.dtype)
    vo_ref[...] = v_ref[...].astype(vo_ref.dtype)
    km_ref[0] = jnp.sum(kr, axis=0, keepdims=True) * (1.0 / MOBA_BLOCK)


def rope_prepass(proj, tab_c, tab_a, tab_b, seq):
    t = proj.shape[0]
    nblk = t // MOBA_BLOCK
    per_seq = seq // MOBA_BLOCK
    row = lambda cb: pl.BlockSpec((MOBA_BLOCK, MOBA_WIDTH), lambda i: (i, cb))
    tab = pl.BlockSpec((MOBA_BLOCK, LANES), lambda i: (i % per_seq, 0))
    out = pl.BlockSpec((MOBA_BLOCK, MOBA_WIDTH), lambda i: (i, 0))
    return pl.pallas_call(
        _rope_kernel,
        grid=(nblk,),
        in_specs=[row(COL_QA // MOBA_WIDTH), row(COL_KA // MOBA_WIDTH), row(COL_VA // MOBA_WIDTH), tab, tab, tab],
        out_specs=[out, out, out, pl.BlockSpec((1, 1, MOBA_WIDTH), lambda i: (i, 0, 0))],
        out_shape=[jax.ShapeDtypeStruct((t, MOBA_WIDTH), jnp.bfloat16)] * 3
        + [jax.ShapeDtypeStruct((nblk, 1, MOBA_WIDTH), jnp.float32)],
        compiler_params=_params(("parallel",)),
        name="rope_kmean",
    )(proj, proj, proj, tab_c, tab_a, tab_b)


def _moba_kernel(q_ref, k_ref, v_ref, kmt_ref, o_ref, *, tq):
    i = pl.program_id(2)
    own = (i * tq) // MOBA_BLOCK
    lane = lax.broadcasted_iota(jnp.int32, (1, LANES), 1)
    upper = lane >= MOBA_HEAD_DIM
    q = q_ref[...]
    zero = jnp.zeros_like(q)
    qs = (jnp.where(upper, zero, q), jnp.where(upper, q, zero))
    scale = MOBA_HEAD_DIM ** -0.5
    nt = (((1,), (1,)), ((), ()))

    sels = []
    for qh in qs:
        gate = jnp.dot(qh.astype(jnp.float32), kmt_ref[0], precision=lax.Precision.HIGHEST,
                       preferred_element_type=jnp.float32)
        g = jnp.where(lane < own, gate, -jnp.inf)
        sel = jnp.zeros(g.shape, jnp.float32)
        for _ in range(MOBA_TOP_BLOCKS):
            m = jnp.max(g, axis=-1, keepdims=True)
            idx = jnp.min(jnp.where(g == m, lane, LANES), axis=-1, keepdims=True)
            hit = lane == idx
            sel = jnp.where(hit & (m > -jnp.inf), 1.0, sel)
            g = jnp.where(hit, -jnp.inf, g)
        sels.append(sel)

    base = pl.multiple_of(own * MOBA_BLOCK, MOBA_BLOCK)
    k_own = k_ref[pl.ds(base, MOBA_BLOCK), :]
    v_own = v_ref[pl.ds(base, MOBA_BLOCK), :]
    q_local = (i * tq) % MOBA_BLOCK + lax.broadcasted_iota(jnp.int32, (tq, MOBA_BLOCK), 0)
    causal = lax.broadcasted_iota(jnp.int32, (tq, MOBA_BLOCK), 1) <= q_local
    carry = []
    for qh in qs:
        s = lax.dot_general(qh, k_own, nt, preferred_element_type=jnp.float32) * scale
        s = jnp.where(causal, s, MASK_VALUE)
        m = jnp.max(s, axis=-1, keepdims=True)
        p = jnp.exp(s - m)
        l = jnp.sum(p, axis=-1, keepdims=True)
        acc = jnp.dot(p.astype(v_own.dtype), v_own, preferred_element_type=jnp.float32)
        carry += [m, l, acc]

    def past_block(n, carry):
        off = pl.multiple_of(n * MOBA_BLOCK, MOBA_BLOCK)
        k_n = k_ref[pl.ds(off, MOBA_BLOCK), :]
        v_n = v_ref[pl.ds(off, MOBA_BLOCK), :]
        new = []
        for h, qh in enumerate(qs):
            m, l, acc = carry[3 * h:3 * h + 3]
            chosen = jnp.max(jnp.where(lane == n, sels[h], 0.0), axis=-1, keepdims=True) > 0.0
            s = lax.dot_general(qh, k_n, nt, preferred_element_type=jnp.float32) * scale
            s = jnp.where(chosen, s, MASK_VALUE)
            m_new = jnp.maximum(m, jnp.max(s, axis=-1, keepdims=True))
            alpha = jnp.exp(m - m_new)
            p = jnp.exp(s - m_new)
            l = alpha * l + jnp.sum(p, axis=-1, keepdims=True)
            acc = alpha * acc + jnp.dot(p.astype(v_n.dtype), v_n, preferred_element_type=jnp.float32)
            new += [m_new, l, acc]
        return tuple(new)

    carry = lax.fori_loop(0, own, past_block, tuple(carry))
    out0 = carry[2] / carry[1]
    out1 = carry[5] / carry[4]
    o_ref[...] = jnp.where(upper, out1, out0).astype(o_ref.dtype)


def moba_attention(q, k, v, kmt, batch, seq, tq):
    t = q.shape[0]
    nq = seq // tq
    pairs = MOBA_WIDTH // LANES
    full = pl.BlockSpec((seq, LANES), lambda b, hp, i: (b, hp))
    return pl.pallas_call(
        functools.partial(_moba_kernel, tq=tq),
        grid=(batch, pairs, nq),
        in_specs=[pl.BlockSpec((tq, LANES), lambda b, hp, i: (b * nq + i, hp)), full, full,
                  pl.BlockSpec((1, LANES, LANES), lambda b, hp, i: (b, hp, 0))],
        out_specs=pl.BlockSpec((tq, LANES), lambda b, hp, i: (b * nq + i, hp)),
        out_shape=jax.ShapeDtypeStruct((t, MOBA_WIDTH), jnp.float32),
        compiler_params=_params(("parallel", "parallel", "arbitrary")),
        name="moba_attention",
    )(q, k, v, kmt)


def _gla_gate_kernel(x_ref, wlg_ref, wup_ref, b_ref, o_ref):
    lg = jnp.dot(x_ref[...], wlg_ref[...], preferred_element_type=jnp.float32)
    z = jnp.dot(lg, wup_ref[...], precision=lax.Precision.HIGHEST, preferred_element_type=jnp.float32) + b_ref[...]
    o_ref[...] = (jnp.minimum(z, 0.0) - jnp.log1p(jnp.exp(-jnp.abs(z)))) * (1.0 / GLA_GATE_TAU)


def gla_gate(xb, w_lg, w_up, b, tm):
    t = xb.shape[0]
    return pl.pallas_call(
        _gla_gate_kernel,
        grid=(t // tm,),
        in_specs=[pl.BlockSpec((tm, D_MODEL), lambda i: (i, 0)),
                  pl.BlockSpec((D_MODEL, LANES), lambda i: (0, 0)),
                  pl.BlockSpec((LANES, GLA_WIDTH), lambda i: (0, 0)),
                  pl.BlockSpec((1, GLA_WIDTH), lambda i: (0, 0))],
        out_specs=pl.BlockSpec((tm, GLA_WIDTH), lambda i: (i, 0)),
        out_shape=jax.ShapeDtypeStruct((t, GLA_WIDTH), jnp.float32),
        compiler_params=_params(("parallel",)),
        name="gla_gate",
    )(xb, w_lg, w_up, b)


def _gla_kernel(q_ref, k_ref, v_ref, r_ref, g_ref, gain_ref, o_ref, state_ref, *, rows):
    @pl.when(pl.program_id(2) == 0)
    def _():
        state_ref[...] = jnp.zeros_like(state_ref)

    c = GLA_CHUNK
    r_i = lax.broadcasted_iota(jnp.int32, (c, c), 0)
    c_i = lax.broadcasted_iota(jnp.int32, (c, c), 1)
    lower = r_i >= c_i
    tri = jnp.where(lower, 1.0, 0.0).astype(jnp.float32)
    nt = (((1,), (1,)), ((), ()))
    tn = (((0,), (0,)), ((), ()))
    bf = jnp.bfloat16
    for j in range(rows // c):
        sl = slice(j * c, (j + 1) * c)
        g = g_ref[sl, :]
        cum = jnp.dot(tri, g, precision=lax.Precision.HIGHEST, preferred_element_type=jnp.float32)
        cum_last = cum[c - 1:c, :]
        q = q_ref[sl, :] * (GLA_KEY_DIM ** -0.5)
        k = k_ref[sl, :]
        v = v_ref[sl, :].astype(bf)
        q_dec = (q * jnp.exp(cum)).astype(bf)
        k_inv = (k * jnp.exp(-cum)).astype(bf)
        k_end = (k * jnp.exp(cum_last - cum)).astype(bf)
        att = lax.dot_general(q_dec, k_inv, nt, preferred_element_type=jnp.float32)
        att = jnp.where(lower, att, 0.0)
        intra = jnp.dot(att.astype(bf), v, preferred_element_type=jnp.float32)
        state = state_ref[...]
        inter = lax.dot_general(q_dec, state.astype(bf), nt, preferred_element_type=jnp.float32)
        d_state = lax.dot_general(v, k_end, tn, preferred_element_type=jnp.float32)
        state_ref[...] = state * jnp.exp(cum_last) + d_state
        o = intra + inter
        mu = jnp.mean(o, axis=-1, keepdims=True)
        var = jnp.mean(jnp.square(o - mu), axis=-1, keepdims=True)
        on = (o - mu) * lax.rsqrt(var + LN_EPS) * gain_ref[...]
        rg = r_ref[sl, :]
        o_ref[sl, :] = (on * (rg * jax.nn.sigmoid(rg))).astype(o_ref.dtype)


def gla(proj, g, gain, batch, seq, rows):
    t = proj.shape[0]
    ns = seq // rows
    col = lambda c0: pl.BlockSpec((rows, LANES), lambda b, h, i: (b * ns + i, c0 // LANES + h))
    head = pl.BlockSpec((rows, LANES), lambda b, h, i: (b * ns + i, h))
    return pl.pallas_call(
        functools.partial(_gla_kernel, rows=rows),
        grid=(batch, GLA_HEADS, ns),
        in_specs=[col(COL_QG), col(COL_KG), col(COL_VG), col(COL_RG), head,
                  pl.BlockSpec((1, LANES), lambda b, h, i: (0, h))],
        out_specs=head,
        out_shape=jax.ShapeDtypeStruct((t, GLA_WIDTH), jnp.float32),
        scratch_shapes=[pltpu.VMEM((LANES, LANES), jnp.float32)],
        compiler_params=_params(("parallel", "parallel", "arbitrary")),
        name="gla",
    )(proj, proj, proj, proj, g, gain)


def _layer_norm(z, g, b):
    mu = jnp.mean(z, axis=-1, keepdims=True)
    var = jnp.mean(jnp.square(z - mu), axis=-1, keepdims=True)
    return (z - mu) * lax.rsqrt(var + LN_EPS) * g + b


def _merge_kernel(ya_ref, yb_ref, ga_ref, gb_ref, x_ref, wa_ref, wb_ref, wo_ref, g_ref, b_ref, xo_ref, xb_ref):
    bf = jnp.bfloat16
    pa = jnp.dot(ya_ref[...].astype(bf), wa_ref[...], preferred_element_type=jnp.float32)
    pb = jnp.dot(yb_ref[...].astype(bf), wb_ref[...], preferred_element_type=jnp.float32)
    merged = jax.nn.sigmoid(ga_ref[...]) * pa + jax.nn.sigmoid(gb_ref[...]) * pb
    mix = jnp.dot(merged.astype(bf), wo_ref[...], preferred_element_type=jnp.float32)
    y = _layer_norm(DEEPNORM_ALPHA * x_ref[...] + mix, g_ref[...], b_ref[...])
    xo_ref[...] = y
    xb_ref[...] = y.astype(bf)


def merge_out(ya, yb, proj, x, wa, wb, wo, g, b, tm):
    t = x.shape[0]
    row = lambda w, cb=0: pl.BlockSpec((tm, w), lambda i: (i, cb))
    const = lambda r, c: pl.BlockSpec((r, c), lambda i: (0, 0))
    return pl.pallas_call(
        _merge_kernel,
        grid=(t // tm,),
        in_specs=[row(MOBA_WIDTH), row(GLA_WIDTH), row(D_MODEL, COL_GATE_A // D_MODEL), row(D_MODEL, COL_GATE_B // D_MODEL),
                  row(D_MODEL), const(MOBA_WIDTH, D_MODEL), const(GLA_WIDTH, D_MODEL), const(D_MODEL, D_MODEL),
                  const(1, D_MODEL), const(1, D_MODEL)],
        out_specs=[row(D_MODEL), row(D_MODEL)],
        out_shape=[jax.ShapeDtypeStruct((t, D_MODEL), jnp.float32), jax.ShapeDtypeStruct((t, D_MODEL), jnp.bfloat16)],
        compiler_params=_params(("parallel",)),
        name="merge_out_ln",
    )(ya, yb, proj, proj, x, wa, wb, wo, g, b)


def _ffn_kernel(xb_ref, x_ref, c_ref, wg_ref, wu_ref, wd_ref, g_ref, b_ref, xo_ref, xbo_ref, acc_ref, *, use_combine):
    e = pl.program_id(1)
    f = pl.program_id(2)

    @pl.when((e == 0) & (f == 0))
    def _():
        acc_ref[...] = jnp.zeros_like(acc_ref)

    h = xb_ref[...]
    gate = jnp.dot(h, wg_ref[0], preferred_element_type=jnp.float32)
    up = jnp.dot(h, wu_ref[0], preferred_element_type=jnp.float32)
    hid = gate * jax.nn.sigmoid(gate) * up
    if use_combine:
        lane = lax.broadcasted_iota(jnp.int32, (1, LANES), 1)
        hid = hid * jnp.sum(jnp.where(lane == e, c_ref[...], 0.0), axis=-1, keepdims=True)
    acc_ref[...] += jnp.dot(hid.astype(jnp.bfloat16), wd_ref[0], preferred_element_type=jnp.float32)

    @pl.when((e == pl.num_programs(1) - 1) & (f == pl.num_programs(2) - 1))
    def _():
        y = _layer_norm(DEEPNORM_ALPHA * x_ref[...] + acc_ref[...], g_ref[...], b_ref[...])
        xo_ref[...] = y
        xbo_ref[...] = y.astype(jnp.bfloat16)


def ffn(xb, x, combine, wg, wu, wd, g, b, tm, tf, use_combine):
    t = x.shape[0]
    n_e, _, fdim = wg.shape
    row = lambda w: pl.BlockSpec((tm, w), lambda i, e, f: (i, 0))
    const = pl.BlockSpec((1, D_MODEL), lambda i, e, f: (0, 0))
    return pl.pallas_call(
        functools.partial(_ffn_kernel, use_combine=use_combine),
        grid=(t // tm, n_e, fdim // tf),
        in_specs=[row(D_MODEL), row(D_MODEL), row(LANES),
                  pl.BlockSpec((1, D_MODEL, tf), lambda i, e, f: (e, 0, f)),
                  pl.BlockSpec((1, D_MODEL, tf), lambda i, e, f: (e, 0, f)),
                  pl.BlockSpec((1, tf, D_MODEL), lambda i, e, f: (e, f, 0)),
                  const, const],
        out_specs=[row(D_MODEL), row(D_MODEL)],
        out_shape=[jax.ShapeDtypeStruct((t, D_MODEL), jnp.float32), jax.ShapeDtypeStruct((t, D_MODEL), jnp.bfloat16)],
        scratch_shapes=[pltpu.VMEM((tm, D_MODEL), jnp.float32)],
        compiler_params=_params(("parallel", "arbitrary", "arbitrary")),
        name="moe_swiglu_ln" if use_combine else "swiglu_ln",
    )(xb, x, combine, wg, wu, wd, g, b)


def _router_kernel(x_ref, w_ref, o_ref):
    logits = jnp.dot(x_ref[...], w_ref[...], precision=lax.Precision.HIGHEST, preferred_element_type=jnp.float32)
    lane = lax.broadcasted_iota(jnp.int32, (1, LANES), 1)
    g = jnp.where(lane < N_EXPERTS, logits, -jnp.inf)
    m1 = jnp.max(g, axis=-1, keepdims=True)
    i1 = jnp.min(jnp.where(g == m1, lane, LANES), axis=-1, keepdims=True)
    g2 = jnp.where(lane == i1, -jnp.inf, g)
    m2 = jnp.max(g2, axis=-1, keepdims=True)
    i2 = jnp.min(jnp.where(g2 == m2, lane, LANES), axis=-1, keepdims=True)
    e2 = jnp.exp(m2 - m1)
    w1 = 1.0 / (1.0 + e2)
    w2 = e2 / (1.0 + e2)
    o_ref[...] = jnp.where(lane == i1, w1, 0.0) + jnp.where(lane == i2, w2, 0.0)


def router(x, w, tm):
    t = x.shape[0]
    return pl.pallas_call(
        _router_kernel,
        grid=(t // tm,),
        in_specs=[pl.BlockSpec((tm, D_MODEL), lambda i: (i, 0)), pl.BlockSpec((D_MODEL, LANES), lambda i: (0, 0))],
        out_specs=pl.BlockSpec((tm, LANES), lambda i: (i, 0)),
        out_shape=jax.ShapeDtypeStruct((t, LANES), jnp.float32),
        compiler_params=_params(("parallel",)),
        name="router",
    )(x, w)


def _rope_tables(seq):
    half = ROPE_DIM // 2
    inv_freq = jnp.power(ROPE_THETA, -jnp.arange(0, ROPE_DIM, 2, dtype=jnp.float32) / ROPE_DIM)
    ang = jnp.arange(seq, dtype=jnp.float32)[:, None] * inv_freq[None, :]
    cos, sin = jnp.cos(ang), jnp.sin(ang)
    ones = jnp.ones((seq, MOBA_HEAD_DIM - ROPE_DIM), jnp.float32)
    zeros = jnp.zeros((seq, MOBA_HEAD_DIM - ROPE_DIM), jnp.float32)
    z8 = jnp.zeros((seq, half), jnp.float32)
    tab_c = jnp.concatenate([cos, cos, ones], axis=1)
    tab_a = jnp.concatenate([-sin, z8, zeros], axis=1)
    tab_b = jnp.concatenate([z8, sin, zeros], axis=1)
    dup = lambda t: jnp.concatenate([t, t], axis=1)
    return dup(tab_c), dup(tab_a), dup(tab_b)


def _split_w_in(w):
    offs = np.cumsum([0, 512, 512, 512, 512, 512, 512, 512, GLA_GATE_RANK, 1024, 1024])
    part = lambda j: w[:, int(offs[j]):int(offs[j + 1])]
    qa, ka, va, qg, kg, vg, rg, lg, gate_a, gate_b = [part(j) for j in range(10)]
    main = jnp.concatenate([gate_a, gate_b, qa, ka, va, qg, kg, vg, rg], axis=1).astype(jnp.bfloat16)
    lg = jnp.pad(lg, ((0, 0), (0, LANES - GLA_GATE_RANK))).astype(jnp.bfloat16)
    return main, lg


def kernel(x, w_in, w_gla_gate_up, b_gla_gate, gla_norm_g, w_branch_a, w_branch_b, w_out, ln_mix_g, ln_mix_b,
           ffn_w_gate, ffn_w_up, ffn_w_down, moe_w_router, moe_w_gate, moe_w_up, moe_w_down, ln_ffn_g, ln_ffn_b):
    batch, seq, d = x.shape
    t = batch * seq
    assert d == D_MODEL and seq % MOBA_BLOCK == 0 and seq // MOBA_BLOCK <= LANES
    bf = jnp.bfloat16
    tab_c, tab_a, tab_b = _rope_tables(seq)
    xf = x.reshape(t, d)
    xb = xf.astype(bf)
    row_tile = math.gcd(t, 1024)
    for layer in range(DEPTH):
        w_main, w_lg = _split_w_in(w_in[layer])
        proj = matmul(xb, w_main, jnp.float32, row_tile, 512)
        q, k, v, kmean = rope_prepass(proj, tab_c, tab_a, tab_b, seq)
        nb = seq // MOBA_BLOCK
        kmt = jnp.pad(kmean.reshape(batch, nb, MOBA_WIDTH).transpose(0, 2, 1), ((0, 0), (0, 0), (0, LANES - nb)))
        y_a = moba_attention(q, k, v, kmt, batch, seq, 128)
        w_up = jnp.pad(w_gla_gate_up[layer], ((0, LANES - GLA_GATE_RANK), (0, 0)))
        g = gla_gate(xb, w_lg, w_up, b_gla_gate[layer].reshape(1, -1), math.gcd(t, 512))
        y_b = gla(proj, g, gla_norm_g[layer].reshape(1, -1), batch, seq, 256)
        xf, xb = merge_out(y_a, y_b, proj, xf, w_branch_a[layer].astype(bf), w_branch_b[layer].astype(bf),
                           w_out[layer].astype(bf), ln_mix_g[layer].reshape(1, -1), ln_mix_b[layer].reshape(1, -1),
                           math.gcd(t, 512))
        j = layer // 2
        ln_g, ln_b = ln_ffn_g[layer].reshape(1, -1), ln_ffn_b[layer].reshape(1, -1)
        if layer % 2 == 0:
            dummy = jnp.zeros((t, LANES), jnp.float32)
            xf, xb = ffn(xb, xf, dummy, ffn_w_gate[j][None].astype(bf), ffn_w_up[j][None].astype(bf),
                         ffn_w_down[j][None].astype(bf), ln_g, ln_b, row_tile, 512, False)
        else:
            w_r = jnp.pad(moe_w_router[j], ((0, 0), (0, LANES - N_EXPERTS)))
            combine = router(xf, w_r, math.gcd(t, 512))
            xf, xb = ffn(xb, xf, combine, moe_w_gate[j].astype(bf), moe_w_up[j].astype(bf),
                         moe_w_down[j].astype(bf), ln_g, ln_b, row_tile, 512, True)
    return xf.reshape(batch, seq, d)
```

```python
import functools
import math

import jax
import jax.numpy as jnp
import numpy as np
from jax import lax
from jax.experimental import pallas as pl
from jax.experimental.pallas import tpu as pltpu

D_MODEL = 1024
DEPTH = 2
MOBA_HEADS = 8
MOBA_HEAD_DIM = 64
MOBA_WIDTH = 512
MOBA_BLOCK = 256
MOBA_TOP_BLOCKS = 3
ROPE_THETA = 500000.0
ROPE_DIM = 16
GLA_HEADS = 4
GLA_KEY_DIM = 128
GLA_WIDTH = 512
GLA_GATE_RANK = 16
GLA_GATE_TAU = 16.0
GLA_CHUNK = 64
FFN_DIM = 3584
N_EXPERTS = 8
DEEPNORM_ALPHA = (2 * DEPTH) ** 0.25
LN_EPS = 1e-5
MASK_VALUE = -1e30

LANES = 128
VMEM_LIMIT = 56 * 1024 * 1024

COL_GATE_A, COL_GATE_B = 0, 1024
COL_QA, COL_KA, COL_VA = 2048, 2560, 3072
COL_QG, COL_KG, COL_VG, COL_RG = 3584, 4096, 4608, 5120
MAIN_COLS = 5632


def _params(sem, **kw):
    return pltpu.CompilerParams(dimension_semantics=sem, vmem_limit_bytes=VMEM_LIMIT, **kw)


def _matmul_kernel(x_ref, w_ref, o_ref):
    o_ref[...] = jnp.dot(x_ref[...], w_ref[...], preferred_element_type=jnp.float32).astype(o_ref.dtype)


def matmul(x, w, out_dtype, tm, tn):
    m, k = x.shape
    n = w.shape[1]
    return pl.pallas_call(
        _matmul_kernel,
        grid=(m // tm, n // tn),
        in_specs=[pl.BlockSpec((tm, k), lambda i, j: (i, 0)),
                  pl.BlockSpec((k, tn), lambda i, j: (0, j))],
        out_specs=pl.BlockSpec((tm, tn), lambda i, j: (i, j)),
        out_shape=jax.ShapeDtypeStruct((m, n), out_dtype),
        compiler_params=_params(("parallel", "arbitrary")),
        name="in_proj",
    )(x, w)


def _rope_kernel(q_ref, k_ref, v_ref, c_ref, a_ref, b_ref, qt_ref, kp_ref, vt_ref, km_ref, *, per_seq):
    half = MOBA_HEAD_DIM
    blk = pl.program_id(0) % per_seq
    lane = lax.broadcasted_iota(jnp.int32, (1, LANES), 1)
    lower = lane < half
    ind = jnp.where(lane == half + blk, 1.0, 0.0)
    zeros = jnp.zeros((LANES - half, MOBA_BLOCK), jnp.float32)
    ones = jnp.ones((LANES - half, MOBA_BLOCK), jnp.float32)

    def rot(t):
        return (t * c_ref[...] + pltpu.roll(t, LANES - ROPE_DIM // 2, 1) * a_ref[...]
                + pltpu.roll(t, ROPE_DIM // 2, 1) * b_ref[...])

    for hp in range(MOBA_WIDTH // LANES):
        sl = slice(hp * LANES, (hp + 1) * LANES)
        q_t = (rot(q_ref[:, sl]) * (MOBA_HEAD_DIM ** -0.5)).T
        v_t = v_ref[:, sl].T
        kr = rot(k_ref[:, sl])
        km = jnp.sum(kr, axis=0, keepdims=True) * (1.0 / MOBA_BLOCK)
        for j, (kh, kmh) in enumerate(((kr, km), (pltpu.roll(kr, half, 1), pltpu.roll(km, half, 1)))):
            h = 2 * hp + j
            rows = slice(j * half, (j + 1) * half)
            qt_ref[0, h] = jnp.concatenate([q_t[rows], zeros], axis=0).astype(qt_ref.dtype)
            vt_ref[0, h, 0] = jnp.concatenate([v_t[rows], ones], axis=0).astype(vt_ref.dtype)
            kp_ref[0, h, 0] = jnp.where(lower, kh, ind).astype(kp_ref.dtype)
            km_ref[0, 0, h:h + 1, :] = jnp.where(lower, kmh, 0.0)


def rope_prepass(proj, tab_c, tab_a, tab_b, batch, seq):
    per_seq = seq // MOBA_BLOCK
    row = lambda cb: pl.BlockSpec((MOBA_BLOCK, MOBA_WIDTH), lambda i: (i, cb))
    tab = pl.BlockSpec((MOBA_BLOCK, LANES), lambda i: (i % per_seq, 0))
    bf = jnp.bfloat16
    return pl.pallas_call(
        functools.partial(_rope_kernel, per_seq=per_seq),
        grid=(batch * per_seq,),
        in_specs=[row(COL_QA // MOBA_WIDTH), row(COL_KA // MOBA_WIDTH), row(COL_VA // MOBA_WIDTH), tab, tab, tab],
        out_specs=[pl.BlockSpec((1, MOBA_HEADS, LANES, MOBA_BLOCK), lambda i: (i // per_seq, 0, 0, i % per_seq)),
                   pl.BlockSpec((1, MOBA_HEADS, 1, MOBA_BLOCK, LANES), lambda i: (i // per_seq, 0, i % per_seq, 0, 0)),
                   pl.BlockSpec((1, MOBA_HEADS, 1, LANES, MOBA_BLOCK), lambda i: (i // per_seq, 0, i % per_seq, 0, 0)),
                   pl.BlockSpec((1, 1, MOBA_HEADS, LANES), lambda i: (i // per_seq, i % per_seq, 0, 0))],
        out_shape=[jax.ShapeDtypeStruct((batch, MOBA_HEADS, LANES, seq), bf),
                   jax.ShapeDtypeStruct((batch, MOBA_HEADS, per_seq, MOBA_BLOCK, LANES), bf),
                   jax.ShapeDtypeStruct((batch, MOBA_HEADS, per_seq, LANES, MOBA_BLOCK), bf),
                   jax.ShapeDtypeStruct((batch, per_seq, MOBA_HEADS, LANES), jnp.float32)],
        compiler_params=_params(("parallel",)),
        name="rope_kmean",
    )(proj, proj, proj, tab_c, tab_a, tab_b)


HEADS_PER_STEP = 4


def _moba_kernel(qt_ref, kp_ref, vt_ref, km_ref, o_ref, qb_ref, *, nb_pad):
    own = pl.program_id(2)
    half = MOBA_HEAD_DIM
    tq = MOBA_BLOCK
    bf = jnp.bfloat16
    blk = lax.broadcasted_iota(jnp.int32, (nb_pad, tq), 0)
    causal = lax.broadcasted_iota(jnp.int32, (tq, tq), 0) <= lax.broadcasted_iota(jnp.int32, (tq, tq), 1)
    carry = []
    for h in range(HEADS_PER_STEP):
        qt = qt_ref[0, h]
        gate = jnp.dot(km_ref[0, h], qt.astype(jnp.float32), precision=lax.Precision.HIGHEST,
                       preferred_element_type=jnp.float32)
        g = jnp.where(blk < own, gate, -jnp.inf)
        sel = jnp.zeros(g.shape, jnp.float32)
        for _ in range(MOBA_TOP_BLOCKS):
            m = jnp.max(g, axis=0, keepdims=True)
            idx = jnp.min(jnp.where(g == m, blk, nb_pad), axis=0, keepdims=True)
            hit = blk == idx
            sel = jnp.where(hit & (m > -jnp.inf), 1.0, sel)
            g = jnp.where(hit, -jnp.inf, g)
        bias = jnp.where(sel > 0.0, 0.0, MASK_VALUE).astype(bf)
        qb_ref[h] = jnp.concatenate([qt[:half], bias, jnp.zeros((LANES - half - nb_pad, tq), bf)], axis=0)

        s = jnp.dot(kp_ref[0, h, own], qt, preferred_element_type=jnp.float32)
        s = jnp.where(causal, s, MASK_VALUE)
        m = jnp.max(s, axis=0, keepdims=True)
        p = jnp.exp(s - m)
        carry += [m, jnp.dot(vt_ref[0, h, own], p.astype(bf), preferred_element_type=jnp.float32)]

    def past_pair(j, carry):
        new = []
        for h in range(HEADS_PER_STEP):
            m, acc = carry[2 * h:2 * h + 2]
            kp = kp_ref[0, h, pl.ds(2 * j, 2)].reshape(2 * MOBA_BLOCK, LANES)
            s = jnp.dot(kp, qb_ref[h], preferred_element_type=jnp.float32)
            m_new = jnp.maximum(m, jnp.max(s, axis=0, keepdims=True))
            p = jnp.exp(s - m_new).astype(bf)
            acc = (jnp.exp(m - m_new) * acc
                   + jnp.dot(vt_ref[0, h, 2 * j], p[:MOBA_BLOCK], preferred_element_type=jnp.float32)
                   + jnp.dot(vt_ref[0, h, 2 * j + 1], p[MOBA_BLOCK:], preferred_element_type=jnp.float32))
            new += [m_new, acc]
        return tuple(new)

    carry = lax.fori_loop(0, (own + 1) // 2, past_pair, tuple(carry))
    outs = [carry[2 * h + 1][:half] / carry[2 * h + 1][half:half + 1] for h in range(HEADS_PER_STEP)]
    o_ref[...] = jnp.concatenate(outs, axis=0).T.astype(o_ref.dtype)


def moba_attention(qt, kp, vt, km, batch, seq):
    nq = seq // MOBA_BLOCK
    nb_pad = km.shape[2]
    hs = HEADS_PER_STEP
    return pl.pallas_call(
        functools.partial(_moba_kernel, nb_pad=nb_pad),
        grid=(batch, MOBA_HEADS // hs, nq),
        in_specs=[pl.BlockSpec((1, hs, LANES, MOBA_BLOCK), lambda b, hp, i: (b, hp, 0, i)),
                  pl.BlockSpec((1, hs, nq, MOBA_BLOCK, LANES), lambda b, hp, i: (b, hp, 0, 0, 0)),
                  pl.BlockSpec((1, hs, nq, LANES, MOBA_BLOCK), lambda b, hp, i: (b, hp, 0, 0, 0)),
                  pl.BlockSpec((1, hs, nb_pad, LANES), lambda b, hp, i: (b, hp, 0, 0))],
        out_specs=pl.BlockSpec((MOBA_BLOCK, hs * MOBA_HEAD_DIM), lambda b, hp, i: (b * nq + i, hp)),
        out_shape=jax.ShapeDtypeStruct((batch * seq, MOBA_WIDTH), jnp.float32),
        scratch_shapes=[pltpu.VMEM((hs, LANES, MOBA_BLOCK), jnp.bfloat16)],
        compiler_params=_params(("parallel", "parallel", "arbitrary")),
        name="moba_attention",
    )(qt, kp, vt, km)


def _gla_gate_kernel(x_ref, wlg_ref, wup_ref, b_ref, o_ref):
    lg = jnp.dot(x_ref[...], wlg_ref[...], preferred_element_type=jnp.float32)
    z = jnp.dot(lg, wup_ref[...], precision=lax.Precision.HIGHEST, preferred_element_type=jnp.float32) + b_ref[...]
    o_ref[...] = (jnp.minimum(z, 0.0) - jnp.log1p(jnp.exp(-jnp.abs(z)))) * (1.0 / GLA_GATE_TAU)


def gla_gate(xb, w_lg, w_up, b, tm):
    t = xb.shape[0]
    return pl.pallas_call(
        _gla_gate_kernel,
        grid=(t // tm,),
        in_specs=[pl.BlockSpec((tm, D_MODEL), lambda i: (i, 0)),
                  pl.BlockSpec((D_MODEL, LANES), lambda i: (0, 0)),
                  pl.BlockSpec((LANES, GLA_WIDTH), lambda i: (0, 0)),
                  pl.BlockSpec((1, GLA_WIDTH), lambda i: (0, 0))],
        out_specs=pl.BlockSpec((tm, GLA_WIDTH), lambda i: (i, 0)),
        out_shape=jax.ShapeDtypeStruct((t, GLA_WIDTH), jnp.float32),
        compiler_params=_params(("parallel",)),
        name="gla_gate",
    )(xb, w_lg, w_up, b)


def _gla_kernel(q_ref, k_ref, v_ref, r_ref, g_ref, gain_ref, o_ref, state_ref, *, rows):
    @pl.when(pl.program_id(2) == 0)
    def _():
        state_ref[...] = jnp.zeros_like(state_ref)

    c = GLA_CHUNK
    r_i = lax.broadcasted_iota(jnp.int32, (c, c), 0)
    c_i = lax.broadcasted_iota(jnp.int32, (c, c), 1)
    lower = r_i >= c_i
    tri = jnp.where(lower, 1.0, 0.0).astype(jnp.float32)
    nt = (((1,), (1,)), ((), ()))
    tn = (((0,), (0,)), ((), ()))
    bf = jnp.bfloat16
    for j in range(rows // c):
        sl = slice(j * c, (j + 1) * c)
        g = g_ref[sl, :]
        cum = jnp.dot(tri, g, precision=lax.Precision.HIGHEST, preferred_element_type=jnp.float32)
        cum_last = cum[c - 1:c, :]
        q = q_ref[sl, :] * (GLA_KEY_DIM ** -0.5)
        k = k_ref[sl, :]
        v = v_ref[sl, :].astype(bf)
        q_dec = (q * jnp.exp(cum)).astype(bf)
        k_inv = (k * jnp.exp(-cum)).astype(bf)
        k_end = (k * jnp.exp(cum_last - cum)).astype(bf)
        att = lax.dot_general(q_dec, k_inv, nt, preferred_element_type=jnp.float32)
        att = jnp.where(lower, att, 0.0)
        intra = jnp.dot(att.astype(bf), v, preferred_element_type=jnp.float32)
        state = state_ref[...]
        inter = lax.dot_general(q_dec, state.astype(bf), nt, preferred_element_type=jnp.float32)
        d_state = lax.dot_general(v, k_end, tn, preferred_element_type=jnp.float32)
        state_ref[...] = state * jnp.exp(cum_last) + d_state
        o = intra + inter
        mu = jnp.mean(o, axis=-1, keepdims=True)
        var = jnp.mean(jnp.square(o - mu), axis=-1, keepdims=True)
        on = (o - mu) * lax.rsqrt(var + LN_EPS) * gain_ref[...]
        rg = r_ref[sl, :]
        o_ref[sl, :] = (on * (rg * jax.nn.sigmoid(rg))).astype(o_ref.dtype)


def gla(proj, g, gain, batch, seq, rows):
    t = proj.shape[0]
    ns = seq // rows
    col = lambda c0: pl.BlockSpec((rows, LANES), lambda b, h, i: (b * ns + i, c0 // LANES + h))
    head = pl.BlockSpec((rows, LANES), lambda b, h, i: (b * ns + i, h))
    return pl.pallas_call(
        functools.partial(_gla_kernel, rows=rows),
        grid=(batch, GLA_HEADS, ns),
        in_specs=[col(COL_QG), col(COL_KG), col(COL_VG), col(COL_RG), head,
                  pl.BlockSpec((1, LANES), lambda b, h, i: (0, h))],
        out_specs=head,
        out_shape=jax.ShapeDtypeStruct((t, GLA_WIDTH), jnp.float32),
        scratch_shapes=[pltpu.VMEM((LANES, LANES), jnp.float32)],
        compiler_params=_params(("parallel", "parallel", "arbitrary")),
        name="gla",
    )(proj, proj, proj, proj, g, gain)


def _layer_norm(z, g, b):
    mu = jnp.mean(z, axis=-1, keepdims=True)
    var = jnp.mean(jnp.square(z - mu), axis=-1, keepdims=True)
    return (z - mu) * lax.rsqrt(var + LN_EPS) * g + b


def _merge_kernel(ya_ref, yb_ref, ga_ref, gb_ref, x_ref, wa_ref, wb_ref, wo_ref, g_ref, b_ref, xo_ref, xb_ref):
    bf = jnp.bfloat16
    pa = jnp.dot(ya_ref[...].astype(bf), wa_ref[...], preferred_element_type=jnp.float32)
    pb = jnp.dot(yb_ref[...].astype(bf), wb_ref[...], preferred_element_type=jnp.float32)
    merged = jax.nn.sigmoid(ga_ref[...]) * pa + jax.nn.sigmoid(gb_ref[...]) * pb
    mix = jnp.dot(merged.astype(bf), wo_ref[...], preferred_element_type=jnp.float32)
    y = _layer_norm(DEEPNORM_ALPHA * x_ref[...] + mix, g_ref[...], b_ref[...])
    xo_ref[...] = y
    xb_ref[...] = y.astype(bf)


def merge_out(ya, yb, proj, x, wa, wb, wo, g, b, tm):
    t = x.shape[0]
    row = lambda w, cb=0: pl.BlockSpec((tm, w), lambda i: (i, cb))
    const = lambda r, c: pl.BlockSpec((r, c), lambda i: (0, 0))
    return pl.pallas_call(
        _merge_kernel,
        grid=(t // tm,),
        in_specs=[row(MOBA_WIDTH), row(GLA_WIDTH), row(D_MODEL, COL_GATE_A // D_MODEL), row(D_MODEL, COL_GATE_B // D_MODEL),
                  row(D_MODEL), const(MOBA_WIDTH, D_MODEL), const(GLA_WIDTH, D_MODEL), const(D_MODEL, D_MODEL),
                  const(1, D_MODEL), const(1, D_MODEL)],
        out_specs=[row(D_MODEL), row(D_MODEL)],
        out_shape=[jax.ShapeDtypeStruct((t, D_MODEL), jnp.float32), jax.ShapeDtypeStruct((t, D_MODEL), jnp.bfloat16)],
        compiler_params=_params(("parallel",)),
        name="merge_out_ln",
    )(ya, yb, proj, proj, x, wa, wb, wo, g, b)


def _ffn_kernel(xb_ref, x_ref, wg_ref, wu_ref, wd_ref, g_ref, b_ref, xo_ref, xbo_ref, acc_ref):
    f = pl.program_id(1)

    @pl.when(f == 0)
    def _():
        acc_ref[...] = jnp.zeros_like(acc_ref)

    h = xb_ref[...]
    gate = jnp.dot(h, wg_ref[...], preferred_element_type=jnp.float32)
    up = jnp.dot(h, wu_ref[...], preferred_element_type=jnp.float32)
    hid = gate * jax.nn.sigmoid(gate) * up
    acc_ref[...] += jnp.dot(hid.astype(jnp.bfloat16), wd_ref[...], preferred_element_type=jnp.float32)

    @pl.when(f == pl.num_programs(1) - 1)
    def _():
        y = _layer_norm(DEEPNORM_ALPHA * x_ref[...] + acc_ref[...], g_ref[...], b_ref[...])
        xo_ref[...] = y
        xbo_ref[...] = y.astype(jnp.bfloat16)


def ffn(xb, x, wg, wu, wd, g, b, tm, tf):
    t = x.shape[0]
    fdim = wg.shape[1]
    row = lambda w: pl.BlockSpec((tm, w), lambda i, f: (i, 0))
    const = pl.BlockSpec((1, D_MODEL), lambda i, f: (0, 0))
    return pl.pallas_call(
        _ffn_kernel,
        grid=(t // tm, fdim // tf),
        in_specs=[row(D_MODEL), row(D_MODEL),
                  pl.BlockSpec((D_MODEL, tf), lambda i, f: (0, f)),
                  pl.BlockSpec((D_MODEL, tf), lambda i, f: (0, f)),
                  pl.BlockSpec((tf, D_MODEL), lambda i, f: (f, 0)),
                  const, const],
        out_specs=[row(D_MODEL), row(D_MODEL)],
        out_shape=[jax.ShapeDtypeStruct((t, D_MODEL), jnp.float32), jax.ShapeDtypeStruct((t, D_MODEL), jnp.bfloat16)],
        scratch_shapes=[pltpu.VMEM((tm, D_MODEL), jnp.float32)],
        compiler_params=_params(("parallel", "arbitrary")),
        name="swiglu_ln",
    )(xb, x, wg, wu, wd, g, b)


ROUTE_E1, ROUTE_E2, ROUTE_W1, ROUTE_W2 = 0, 1, 2, 3


def _router_kernel(x_ref, w_ref, o_ref):
    logits = jnp.dot(x_ref[...], w_ref[...], precision=lax.Precision.HIGHEST, preferred_element_type=jnp.float32)
    lane = lax.broadcasted_iota(jnp.int32, (1, LANES), 1)
    g = jnp.where(lane < N_EXPERTS, logits, -jnp.inf)
    m1 = jnp.max(g, axis=-1, keepdims=True)
    i1 = jnp.min(jnp.where(g == m1, lane, LANES), axis=-1, keepdims=True)
    g2 = jnp.where(lane == i1, -jnp.inf, g)
    m2 = jnp.max(g2, axis=-1, keepdims=True)
    i2 = jnp.min(jnp.where(g2 == m2, lane, LANES), axis=-1, keepdims=True)
    e2 = jnp.exp(m2 - m1)
    w1 = 1.0 / (1.0 + e2)
    w2 = e2 / (1.0 + e2)
    out = jnp.where(lane == ROUTE_E1, i1.astype(jnp.float32), 0.0)
    out = jnp.where(lane == ROUTE_E2, i2.astype(jnp.float32), out)
    out = jnp.where(lane == ROUTE_W1, w1, out)
    o_ref[...] = jnp.where(lane == ROUTE_W2, w2, out)


def router(x, w, tm):
    t = x.shape[0]
    return pl.pallas_call(
        _router_kernel,
        grid=(t // tm,),
        in_specs=[pl.BlockSpec((tm, D_MODEL), lambda i: (i, 0)), pl.BlockSpec((D_MODEL, LANES), lambda i: (0, 0))],
        out_specs=pl.BlockSpec((tm, LANES), lambda i: (i, 0)),
        out_shape=jax.ShapeDtypeStruct((t, LANES), jnp.float32),
        compiler_params=_params(("parallel",)),
        name="router",
    )(x, w)


def _routing_tables(route, tm):
    t = route.shape[0]
    experts = route[:, ROUTE_E1:ROUTE_E2 + 1].astype(jnp.int32).reshape(-1)
    onehot = (experts[:, None] == jnp.arange(N_EXPERTS, dtype=jnp.int32)[None, :]).astype(jnp.int32)
    csum = jnp.cumsum(onehot, axis=0)
    rank = jnp.sum((csum - onehot) * onehot, axis=1)
    size = -(-csum[-1] // tm) * tm
    end = jnp.cumsum(size)
    pos = jnp.sum(onehot * (end - size)[None, :], axis=1) + rank
    n_tiles = 2 * t // tm + N_EXPERTS
    tok = jnp.zeros((n_tiles * tm,), jnp.int32).at[pos].set(jnp.arange(2 * t, dtype=jnp.int32) // 2)
    tile_start = jnp.arange(n_tiles, dtype=jnp.int32) * tm
    tile_expert = jnp.minimum(jnp.sum((tile_start[:, None] >= end[None, :]).astype(jnp.int32), axis=1), N_EXPERTS - 1)
    n_used = end[-1] // tm
    return tok.reshape(n_tiles, tm), pos.reshape(t, 2), tile_expert, n_used.reshape(1)


SUBLANES = 8


def _issue_row_copies(slot, n_rows, idx_ref, src_hbm, dst_ref, sem):
    for s in range(2):
        @pl.when(slot == s)
        def _():
            def body(j, c):
                base = pl.multiple_of(j * SUBLANES, SUBLANES)
                for u in range(SUBLANES):
                    pltpu.make_async_copy(src_hbm.at[pl.ds(idx_ref[s, base + u], 1), :],
                                          dst_ref.at[s, pl.ds(base + u, 1), :], sem.at[s]).start()
                return c
            lax.fori_loop(0, n_rows // SUBLANES, body, 0)


def _moe_kernel(te_ref, nu_ref, tok_hbm, x_hbm, wg_ref, wu_ref, wd_ref, y_ref,
                idx_ref, xg_ref, xb_ref, acc_ref, idx_sem, row_sem, *, tm):
    i = pl.program_id(0)
    f = pl.program_id(1)
    nt = pl.num_programs(0)
    slot = i % 2

    def idx_copy(tile, s):
        return pltpu.make_async_copy(tok_hbm.at[tile], idx_ref.at[s], idx_sem.at[s])

    def gather_rows(s):
        _issue_row_copies(s, tm, idx_ref, x_hbm, xg_ref, row_sem)

    def rows_done(s):
        return pltpu.make_async_copy(x_hbm.at[pl.ds(0, tm), :], xg_ref.at[s], row_sem.at[s])

    @pl.when(f == 0)
    def _():
        @pl.when(i == 0)
        def _():
            idx_copy(0, 0).start()
            idx_copy(0, 0).wait()
            gather_rows(0)
            idx_copy(1, 1).start()

        @pl.when(i + 1 < nt)
        def _():
            idx_copy(i + 1, 1 - slot).wait()
            gather_rows(1 - slot)

        rows_done(slot).wait()
        xb_ref[...] = xg_ref[slot].astype(jnp.bfloat16)
        acc_ref[...] = jnp.zeros_like(acc_ref)

    @pl.when((f == 1) & (i + 2 < nt))
    def _():
        idx_copy(i + 2, slot).start()

    @pl.when(i < nu_ref[0])
    def _():
        h = xb_ref[...]
        gate = jnp.dot(h, wg_ref[0], preferred_element_type=jnp.float32)
        up = jnp.dot(h, wu_ref[0], preferred_element_type=jnp.float32)
        hid = gate * jax.nn.sigmoid(gate) * up
        acc_ref[...] += jnp.dot(hid.astype(jnp.bfloat16), wd_ref[0], preferred_element_type=jnp.float32)

    @pl.when(f == pl.num_programs(1) - 1)
    def _():
        y_ref[...] = acc_ref[...]


def moe_experts(tile_expert, n_used, tok, x, wg, wu, wd, tm, tf):
    n_tiles = tok.shape[0]
    fdim = wg.shape[2]
    assert fdim // tf >= 2 and n_tiles >= 2
    grid_spec = pltpu.PrefetchScalarGridSpec(
        num_scalar_prefetch=2,
        grid=(n_tiles, fdim // tf),
        in_specs=[pl.BlockSpec(memory_space=pl.ANY), pl.BlockSpec(memory_space=pl.ANY),
                  pl.BlockSpec((1, D_MODEL, tf), lambda i, f, te, nu: (te[i], 0, f)),
                  pl.BlockSpec((1, D_MODEL, tf), lambda i, f, te, nu: (te[i], 0, f)),
                  pl.BlockSpec((1, tf, D_MODEL), lambda i, f, te, nu: (te[i], f, 0))],
        out_specs=pl.BlockSpec((tm, D_MODEL), lambda i, f, te, nu: (i, 0)),
        scratch_shapes=[pltpu.SMEM((2, tm), jnp.int32), pltpu.VMEM((2, tm, D_MODEL), jnp.float32),
                        pltpu.VMEM((tm, D_MODEL), jnp.bfloat16), pltpu.VMEM((tm, D_MODEL), jnp.float32),
                        pltpu.SemaphoreType.DMA((2,)), pltpu.SemaphoreType.DMA((2,))])
    return pl.pallas_call(
        functools.partial(_moe_kernel, tm=tm),
        grid_spec=grid_spec,
        out_shape=jax.ShapeDtypeStruct((n_tiles * tm, D_MODEL), jnp.float32),
        compiler_params=_params(("arbitrary", "arbitrary"), disable_bounds_checks=True),
        name="moe_experts",
    )(tile_expert, n_used, tok, x, wg, wu, wd)


def _combine_kernel(pos_hbm, y_hbm, route_ref, x_ref, g_ref, b_ref, xo_ref, idx_ref, rows_ref, idx_sem, row_sem, *, tc):
    i = pl.program_id(0)
    nt = pl.num_programs(0)
    slot = i % 2

    def idx_copy(tile, s):
        return pltpu.make_async_copy(pos_hbm.at[tile], idx_ref.at[s], idx_sem.at[s])

    def gather_rows(s):
        _issue_row_copies(s, 2 * tc, idx_ref, y_hbm, rows_ref, row_sem)

    @pl.when(i == 0)
    def _():
        idx_copy(0, 0).start()
        idx_copy(0, 0).wait()
        gather_rows(0)
        idx_copy(1, 1).start()

    @pl.when(i + 1 < nt)
    def _():
        idx_copy(i + 1, 1 - slot).wait()
        gather_rows(1 - slot)

    pltpu.make_async_copy(y_hbm.at[pl.ds(0, 2 * tc), :], rows_ref.at[slot], row_sem.at[slot]).wait()

    @pl.when(i + 2 < nt)
    def _():
        idx_copy(i + 2, slot).start()

    route = route_ref[...]
    w1 = route[:, ROUTE_W1:ROUTE_W1 + 1]
    w2 = route[:, ROUTE_W2:ROUTE_W2 + 1]
    f = w1 * rows_ref[slot, :tc, :] + w2 * rows_ref[slot, tc:, :]
    xo_ref[...] = _layer_norm(DEEPNORM_ALPHA * x_ref[...] + f, g_ref[...], b_ref[...])


def moe_combine(pos, y, route, x, g, b, tc):
    t = x.shape[0]
    n_tiles = t // tc
    assert n_tiles >= 2
    pos_tiles = pos.reshape(n_tiles, tc, 2).transpose(0, 2, 1).reshape(n_tiles, 2 * tc)
    const = pl.BlockSpec((1, D_MODEL), lambda i: (0, 0))
    return pl.pallas_call(
        functools.partial(_combine_kernel, tc=tc),
        grid=(n_tiles,),
        in_specs=[pl.BlockSpec(memory_space=pl.ANY), pl.BlockSpec(memory_space=pl.ANY),
                  pl.BlockSpec((tc, LANES), lambda i: (i, 0)), pl.BlockSpec((tc, D_MODEL), lambda i: (i, 0)),
                  const, const],
        out_specs=pl.BlockSpec((tc, D_MODEL), lambda i: (i, 0)),
        out_shape=jax.ShapeDtypeStruct((t, D_MODEL), jnp.float32),
        scratch_shapes=[pltpu.SMEM((2, 2 * tc), jnp.int32), pltpu.VMEM((2, 2 * tc, D_MODEL), jnp.float32),
                        pltpu.SemaphoreType.DMA((2,)), pltpu.SemaphoreType.DMA((2,))],
        compiler_params=_params(("arbitrary",), disable_bounds_checks=True),
        name="moe_combine_ln",
    )(pos_tiles, y, route, x, g, b)


def _rope_tables(seq):
    half = ROPE_DIM // 2
    inv_freq = jnp.power(ROPE_THETA, -jnp.arange(0, ROPE_DIM, 2, dtype=jnp.float32) / ROPE_DIM)
    ang = jnp.arange(seq, dtype=jnp.float32)[:, None] * inv_freq[None, :]
    cos, sin = jnp.cos(ang), jnp.sin(ang)
    ones = jnp.ones((seq, MOBA_HEAD_DIM - ROPE_DIM), jnp.float32)
    zeros = jnp.zeros((seq, MOBA_HEAD_DIM - ROPE_DIM), jnp.float32)
    z8 = jnp.zeros((seq, half), jnp.float32)
    tab_c = jnp.concatenate([cos, cos, ones], axis=1)
    tab_a = jnp.concatenate([-sin, z8, zeros], axis=1)
    tab_b = jnp.concatenate([z8, sin, zeros], axis=1)
    dup = lambda t: jnp.concatenate([t, t], axis=1)
    return dup(tab_c), dup(tab_a), dup(tab_b)


def _split_w_in(w):
    offs = np.cumsum([0, 512, 512, 512, 512, 512, 512, 512, GLA_GATE_RANK, 1024, 1024])
    part = lambda j: w[:, int(offs[j]):int(offs[j + 1])]
    qa, ka, va, qg, kg, vg, rg, lg, gate_a, gate_b = [part(j) for j in range(10)]
    main = jnp.concatenate([gate_a, gate_b, qa, ka, va, qg, kg, vg, rg], axis=1).astype(jnp.bfloat16)
    lg = jnp.pad(lg, ((0, 0), (0, LANES - GLA_GATE_RANK))).astype(jnp.bfloat16)
    return main, lg


def kernel(x, w_in, w_gla_gate_up, b_gla_gate, gla_norm_g, w_branch_a, w_branch_b, w_out, ln_mix_g, ln_mix_b,
           ffn_w_gate, ffn_w_up, ffn_w_down, moe_w_router, moe_w_gate, moe_w_up, moe_w_down, ln_ffn_g, ln_ffn_b):
    batch, seq, d = x.shape
    t = batch * seq
    bf16_rows = 16
    nb_pad = -(-(seq // MOBA_BLOCK) // bf16_rows) * bf16_rows
    assert d == D_MODEL and seq % MOBA_BLOCK == 0 and nb_pad <= LANES - MOBA_HEAD_DIM
    bf = jnp.bfloat16
    tab_c, tab_a, tab_b = _rope_tables(seq)
    xf = x.reshape(t, d)
    xb = xf.astype(bf)
    row_tile = math.gcd(t, 1024)
    for layer in range(DEPTH):
        w_main, w_lg = _split_w_in(w_in[layer])
        proj = matmul(xb, w_main, jnp.float32, row_tile, 512)
        qt, kp, vt, kmean = rope_prepass(proj, tab_c, tab_a, tab_b, batch, seq)
        km = jnp.pad(kmean.transpose(0, 2, 1, 3), ((0, 0), (0, 0), (0, nb_pad - seq // MOBA_BLOCK), (0, 0)))
        y_a = moba_attention(qt, kp, vt, km, batch, seq)
        w_up = jnp.pad(w_gla_gate_up[layer], ((0, LANES - GLA_GATE_RANK), (0, 0)))
        g = gla_gate(xb, w_lg, w_up, b_gla_gate[layer].reshape(1, -1), math.gcd(t, 512))
        y_b = gla(proj, g, gla_norm_g[layer].reshape(1, -1), batch, seq, 256)
        xf, xb = merge_out(y_a, y_b, proj, xf, w_branch_a[layer].astype(bf), w_branch_b[layer].astype(bf),
                           w_out[layer].astype(bf), ln_mix_g[layer].reshape(1, -1), ln_mix_b[layer].reshape(1, -1),
                           math.gcd(t, 512))
        j = layer // 2
        ln_g, ln_b = ln_ffn_g[layer].reshape(1, -1), ln_ffn_b[layer].reshape(1, -1)
        if layer % 2 == 0:
            xf, xb = ffn(xb, xf, ffn_w_gate[j].astype(bf), ffn_w_up[j].astype(bf), ffn_w_down[j].astype(bf),
                         ln_g, ln_b, row_tile, 512)
        else:
            w_r = jnp.pad(moe_w_router[j], ((0, 0), (0, LANES - N_EXPERTS)))
            route = router(xf, w_r, math.gcd(t, 512))
            tok, pos, tile_expert, n_used = _routing_tables(route, row_tile)
            y = moe_experts(tile_expert, n_used, tok, xf, moe_w_gate[j].astype(bf), moe_w_up[j].astype(bf),
                            moe_w_down[j].astype(bf), row_tile, 512)
            xf = moe_combine(pos, y, route, xf, ln_g, ln_b, math.gcd(t, 512))
    return xf.reshape(batch, seq, d)
```

```python
import functools
import math

import jax
import jax.numpy as jnp
import numpy as np
from jax import lax
from jax.experimental import pallas as pl
from jax.experimental.pallas import tpu as pltpu

D_MODEL = 1024
DEPTH = 2
MOBA_HEADS = 8
MOBA_HEAD_DIM = 64
MOBA_WIDTH = 512
MOBA_BLOCK = 256
MOBA_TOP_BLOCKS = 3
ROPE_THETA = 500000.0
ROPE_DIM = 16
GLA_HEADS = 4
GLA_KEY_DIM = 128
GLA_WIDTH = 512
GLA_GATE_RANK = 16
GLA_GATE_TAU = 16.0
GLA_CHUNK = 64
FFN_DIM = 3584
N_EXPERTS = 8
DEEPNORM_ALPHA = (2 * DEPTH) ** 0.25
LN_EPS = 1e-5
MASK_VALUE = -1e30

LANES = 128
VMEM_LIMIT = 56 * 1024 * 1024

COL_GATE_A, COL_GATE_B = 0, 1024
COL_QA, COL_KA, COL_VA = 2048, 2560, 3072
COL_QG, COL_KG, COL_VG, COL_RG = 3584, 4096, 4608, 5120
MAIN_COLS = 5632


def _params(sem, **kw):
    return pltpu.CompilerParams(dimension_semantics=sem, vmem_limit_bytes=VMEM_LIMIT, **kw)


def _matmul_kernel(x_ref, w_ref, o_ref):
    o_ref[...] = jnp.dot(x_ref[...], w_ref[...], preferred_element_type=jnp.float32).astype(o_ref.dtype)


def matmul(x, w, out_dtype, tm, tn):
    m, k = x.shape
    n = w.shape[1]
    return pl.pallas_call(
        _matmul_kernel,
        grid=(m // tm, n // tn),
        in_specs=[pl.BlockSpec((tm, k), lambda i, j: (i, 0)),
                  pl.BlockSpec((k, tn), lambda i, j: (0, j))],
        out_specs=pl.BlockSpec((tm, tn), lambda i, j: (i, j)),
        out_shape=jax.ShapeDtypeStruct((m, n), out_dtype),
        compiler_params=_params(("parallel", "arbitrary")),
        name="in_proj",
    )(x, w)


def _rope_kernel(q_ref, k_ref, v_ref, c_ref, a_ref, b_ref, qt_ref, kp_ref, vt_ref, km_ref, *, per_seq):
    half = MOBA_HEAD_DIM
    blk = pl.program_id(0) % per_seq
    lane = lax.broadcasted_iota(jnp.int32, (1, LANES), 1)
    lower = lane < half
    ind = jnp.where(lane == half + blk, 1.0, 0.0)
    zeros = jnp.zeros((LANES - half, MOBA_BLOCK), jnp.float32)
    ones = jnp.ones((LANES - half, MOBA_BLOCK), jnp.float32)

    def rot(t):
        return (t * c_ref[...] + pltpu.roll(t, LANES - ROPE_DIM // 2, 1) * a_ref[...]
                + pltpu.roll(t, ROPE_DIM // 2, 1) * b_ref[...])

    for hp in range(MOBA_WIDTH // LANES):
        sl = slice(hp * LANES, (hp + 1) * LANES)
        f32 = jnp.float32
        q_t = (rot(q_ref[:, sl].astype(f32)) * (MOBA_HEAD_DIM ** -0.5)).T
        v_t = v_ref[:, sl].astype(f32).T
        kr = rot(k_ref[:, sl].astype(f32))
        km = jnp.sum(kr, axis=0, keepdims=True) * (1.0 / MOBA_BLOCK)
        for j, (kh, kmh) in enumerate(((kr, km), (pltpu.roll(kr, half, 1), pltpu.roll(km, half, 1)))):
            h = 2 * hp + j
            rows = slice(j * half, (j + 1) * half)
            qt_ref[0, h] = jnp.concatenate([q_t[rows], zeros], axis=0).astype(qt_ref.dtype)
            vt_ref[0, h, 0] = jnp.concatenate([v_t[rows], ones], axis=0).astype(vt_ref.dtype)
            kp_ref[0, h, 0] = jnp.where(lower, kh, ind).astype(kp_ref.dtype)
            km_ref[0, 0, h:h + 1, :] = jnp.where(lower, kmh, 0.0)


def rope_prepass(proj, tab_c, tab_a, tab_b, batch, seq):
    per_seq = seq // MOBA_BLOCK
    row = lambda cb: pl.BlockSpec((MOBA_BLOCK, MOBA_WIDTH), lambda i: (i, cb))
    tab = pl.BlockSpec((MOBA_BLOCK, LANES), lambda i: (i % per_seq, 0))
    bf = jnp.bfloat16
    return pl.pallas_call(
        functools.partial(_rope_kernel, per_seq=per_seq),
        grid=(batch * per_seq,),
        in_specs=[row(COL_QA // MOBA_WIDTH), row(COL_KA // MOBA_WIDTH), row(COL_VA // MOBA_WIDTH), tab, tab, tab],
        out_specs=[pl.BlockSpec((1, MOBA_HEADS, LANES, MOBA_BLOCK), lambda i: (i // per_seq, 0, 0, i % per_seq)),
                   pl.BlockSpec((1, MOBA_HEADS, 1, MOBA_BLOCK, LANES), lambda i: (i // per_seq, 0, i % per_seq, 0, 0)),
                   pl.BlockSpec((1, MOBA_HEADS, 1, LANES, MOBA_BLOCK), lambda i: (i // per_seq, 0, i % per_seq, 0, 0)),
                   pl.BlockSpec((1, 1, MOBA_HEADS, LANES), lambda i: (i // per_seq, i % per_seq, 0, 0))],
        out_shape=[jax.ShapeDtypeStruct((batch, MOBA_HEADS, LANES, seq), bf),
                   jax.ShapeDtypeStruct((batch, MOBA_HEADS, per_seq, MOBA_BLOCK, LANES), bf),
                   jax.ShapeDtypeStruct((batch, MOBA_HEADS, per_seq, LANES, MOBA_BLOCK), bf),
                   jax.ShapeDtypeStruct((batch, per_seq, MOBA_HEADS, LANES), jnp.float32)],
        compiler_params=_params(("parallel",)),
        name="rope_kmean",
    )(proj, proj, proj, tab_c, tab_a, tab_b)


HEADS_PER_STEP = 4


def _moba_kernel(qt_ref, kp_ref, vt_ref, km_ref, o_ref, qb_ref, *, nb_pad):
    own = pl.program_id(2)
    half = MOBA_HEAD_DIM
    tq = MOBA_BLOCK
    bf = jnp.bfloat16
    blk = lax.broadcasted_iota(jnp.int32, (nb_pad, tq), 0)
    causal = lax.broadcasted_iota(jnp.int32, (tq, tq), 0) <= lax.broadcasted_iota(jnp.int32, (tq, tq), 1)

    def weights(h, j, m):
        kp = kp_ref[0, h, pl.ds(2 * j, 2)].reshape(2 * MOBA_BLOCK, LANES)
        s = jnp.dot(kp, qb_ref[h], preferred_element_type=jnp.float32)
        m_new = jnp.maximum(m, jnp.max(s, axis=0, keepdims=True))
        return m_new, jnp.exp(s - m_new).astype(bf), jnp.exp(m - m_new)

    def absorb(h, j, acc, p, alpha):
        return (alpha * acc
                + jnp.dot(vt_ref[0, h, 2 * j], p[:MOBA_BLOCK], preferred_element_type=jnp.float32)
                + jnp.dot(vt_ref[0, h, 2 * j + 1], p[MOBA_BLOCK:], preferred_element_type=jnp.float32))

    carry = []
    for h in range(HEADS_PER_STEP):
        qt = qt_ref[0, h]
        gate = jnp.dot(km_ref[0, h], qt.astype(jnp.float32), precision=lax.Precision.HIGHEST,
                       preferred_element_type=jnp.float32)
        g = jnp.where(blk < own, gate, -jnp.inf)
        sel = jnp.zeros(g.shape, jnp.float32)
        for _ in range(MOBA_TOP_BLOCKS):
            m = jnp.max(g, axis=0, keepdims=True)
            idx = jnp.min(jnp.where(g == m, blk, nb_pad), axis=0, keepdims=True)
            hit = blk == idx
            sel = jnp.where(hit & (m > -jnp.inf), 1.0, sel)
            g = jnp.where(hit, -jnp.inf, g)
        bias = jnp.where(sel > 0.0, 0.0, MASK_VALUE).astype(bf)
        qb_ref[h] = jnp.concatenate([qt[:half], bias, jnp.zeros((LANES - half - nb_pad, tq), bf)], axis=0)

        s = jnp.dot(kp_ref[0, h, own], qt, preferred_element_type=jnp.float32)
        s = jnp.where(causal, s, MASK_VALUE)
        m = jnp.max(s, axis=0, keepdims=True)
        p = jnp.exp(s - m)
        acc = jnp.dot(vt_ref[0, h, own], p.astype(bf), preferred_element_type=jnp.float32)
        carry += [acc, *weights(h, 0, m)]

    def step(j, carry):
        new = []
        for h in range(HEADS_PER_STEP):
            acc, m, p, alpha = carry[4 * h:4 * h + 4]
            new += [absorb(h, j - 1, acc, p, alpha), *weights(h, j, m)]
        return tuple(new)

    n_pairs = jnp.maximum((own + 1) // 2, 1)
    carry = lax.fori_loop(1, n_pairs, step, tuple(carry))
    outs = []
    for h in range(HEADS_PER_STEP):
        acc, _, p, alpha = carry[4 * h:4 * h + 4]
        acc = absorb(h, n_pairs - 1, acc, p, alpha)
        outs.append(acc[:half] / acc[half:half + 1])
    o_ref[...] = jnp.concatenate(outs, axis=0).T.astype(o_ref.dtype)


def moba_attention(qt, kp, vt, km, batch, seq):
    nq = seq // MOBA_BLOCK
    nb_pad = km.shape[2]
    hs = HEADS_PER_STEP
    return pl.pallas_call(
        functools.partial(_moba_kernel, nb_pad=nb_pad),
        grid=(batch, MOBA_HEADS // hs, nq),
        in_specs=[pl.BlockSpec((1, hs, LANES, MOBA_BLOCK), lambda b, hp, i: (b, hp, 0, i)),
                  pl.BlockSpec((1, hs, nq, MOBA_BLOCK, LANES), lambda b, hp, i: (b, hp, 0, 0, 0)),
                  pl.BlockSpec((1, hs, nq, LANES, MOBA_BLOCK), lambda b, hp, i: (b, hp, 0, 0, 0)),
                  pl.BlockSpec((1, hs, nb_pad, LANES), lambda b, hp, i: (b, hp, 0, 0))],
        out_specs=pl.BlockSpec((MOBA_BLOCK, hs * MOBA_HEAD_DIM), lambda b, hp, i: (b * nq + i, hp)),
        out_shape=jax.ShapeDtypeStruct((batch * seq, MOBA_WIDTH), jnp.bfloat16),
        scratch_shapes=[pltpu.VMEM((hs, LANES, MOBA_BLOCK), jnp.bfloat16)],
        compiler_params=_params(("parallel", "parallel", "arbitrary")),
        name="moba_attention",
    )(qt, kp, vt, km)


def _gla_gate_kernel(x_ref, wlg_ref, wup_ref, b_ref, o_ref):
    lg = jnp.dot(x_ref[...], wlg_ref[...], preferred_element_type=jnp.float32)
    z = jnp.dot(lg, wup_ref[...], precision=lax.Precision.HIGHEST, preferred_element_type=jnp.float32) + b_ref[...]
    o_ref[...] = (jnp.minimum(z, 0.0) - jnp.log1p(jnp.exp(-jnp.abs(z)))) * (1.0 / GLA_GATE_TAU)


def gla_gate(xb, w_lg, w_up, b, tm):
    t = xb.shape[0]
    return pl.pallas_call(
        _gla_gate_kernel,
        grid=(t // tm,),
        in_specs=[pl.BlockSpec((tm, D_MODEL), lambda i: (i, 0)),
                  pl.BlockSpec((D_MODEL, LANES), lambda i: (0, 0)),
                  pl.BlockSpec((LANES, GLA_WIDTH), lambda i: (0, 0)),
                  pl.BlockSpec((1, GLA_WIDTH), lambda i: (0, 0))],
        out_specs=pl.BlockSpec((tm, GLA_WIDTH), lambda i: (i, 0)),
        out_shape=jax.ShapeDtypeStruct((t, GLA_WIDTH), jnp.float32),
        compiler_params=_params(("parallel",)),
        name="gla_gate",
    )(xb, w_lg, w_up, b)


def _gla_kernel(q_ref, k_ref, v_ref, r_ref, g_ref, gain_ref, o_ref, state_ref, *, rows):
    @pl.when(pl.program_id(1) == 0)
    def _():
        state_ref[...] = jnp.zeros_like(state_ref)

    c = GLA_CHUNK
    r_i = lax.broadcasted_iota(jnp.int32, (c, c), 0)
    c_i = lax.broadcasted_iota(jnp.int32, (c, c), 1)
    lower = r_i >= c_i
    tri = jnp.where(lower, 1.0, 0.0).astype(jnp.float32)
    nt = (((1,), (1,)), ((), ()))
    tn = (((0,), (0,)), ((), ()))
    bf = jnp.bfloat16
    f32 = jnp.float32
    for j in range(rows // c):
        sl = slice(j * c, (j + 1) * c)
        for h in range(GLA_HEADS):
            hl = slice(h * LANES, (h + 1) * LANES)
            cum = jnp.dot(tri, g_ref[sl, hl], precision=lax.Precision.HIGHEST, preferred_element_type=f32)
            cum_last = cum[c - 1:c, :]
            q = q_ref[sl, hl].astype(f32) * (GLA_KEY_DIM ** -0.5)
            k = k_ref[sl, hl].astype(f32)
            v = v_ref[sl, hl]
            q_dec = (q * jnp.exp(cum)).astype(bf)
            k_inv = (k * jnp.exp(-cum)).astype(bf)
            k_end = (k * jnp.exp(cum_last - cum)).astype(bf)
            att = lax.dot_general(q_dec, k_inv, nt, preferred_element_type=f32)
            att = jnp.where(lower, att, 0.0)
            intra = jnp.dot(att.astype(bf), v, preferred_element_type=f32)
            state = state_ref[h]
            inter = lax.dot_general(q_dec, state.astype(bf), nt, preferred_element_type=f32)
            d_state = lax.dot_general(v, k_end, tn, preferred_element_type=f32)
            state_ref[h] = state * jnp.exp(cum_last) + d_state
            o = intra + inter
            mu = jnp.mean(o, axis=-1, keepdims=True)
            var = jnp.mean(jnp.square(o - mu), axis=-1, keepdims=True)
            on = (o - mu) * lax.rsqrt(var + LN_EPS) * gain_ref[:, hl]
            rg = r_ref[sl, hl].astype(f32)
            o_ref[sl, hl] = (on * (rg * jax.nn.sigmoid(rg))).astype(o_ref.dtype)


def gla(proj, g, gain, batch, seq, rows):
    t = proj.shape[0]
    assert seq % rows == 0 and rows % GLA_CHUNK == 0
    ns = seq // rows
    col = lambda c0: pl.BlockSpec((rows, GLA_WIDTH), lambda b, i: (b * ns + i, c0 // GLA_WIDTH))
    return pl.pallas_call(
        functools.partial(_gla_kernel, rows=rows),
        grid=(batch, ns),
        in_specs=[col(COL_QG), col(COL_KG), col(COL_VG), col(COL_RG), col(0),
                  pl.BlockSpec((1, GLA_WIDTH), lambda b, i: (0, 0))],
        out_specs=col(0),
        out_shape=jax.ShapeDtypeStruct((t, GLA_WIDTH), jnp.bfloat16),
        scratch_shapes=[pltpu.VMEM((GLA_HEADS, LANES, LANES), jnp.float32)],
        compiler_params=_params(("parallel", "arbitrary")),
        name="gla",
    )(proj, proj, proj, proj, g, gain)


def _layer_norm(z, g, b):
    mu = jnp.mean(z, axis=-1, keepdims=True)
    var = jnp.mean(jnp.square(z - mu), axis=-1, keepdims=True)
    return (z - mu) * lax.rsqrt(var + LN_EPS) * g + b


def _merge_kernel(ya_ref, yb_ref, ga_ref, gb_ref, x_ref, wa_ref, wb_ref, wo_ref, g_ref, b_ref, xo_ref, xb_ref):
    bf = jnp.bfloat16
    f32 = jnp.float32
    pa = jnp.dot(ya_ref[...], wa_ref[...], preferred_element_type=f32)
    pb = jnp.dot(yb_ref[...], wb_ref[...], preferred_element_type=f32)
    merged = jax.nn.sigmoid(ga_ref[...].astype(f32)) * pa + jax.nn.sigmoid(gb_ref[...].astype(f32)) * pb
    mix = jnp.dot(merged.astype(bf), wo_ref[...], preferred_element_type=jnp.float32)
    y = _layer_norm(DEEPNORM_ALPHA * x_ref[...] + mix, g_ref[...], b_ref[...])
    xo_ref[...] = y
    xb_ref[...] = y.astype(bf)


def merge_out(ya, yb, proj, x, wa, wb, wo, g, b, tm):
    t = x.shape[0]
    row = lambda w, cb=0: pl.BlockSpec((tm, w), lambda i: (i, cb))
    const = lambda r, c: pl.BlockSpec((r, c), lambda i: (0, 0))
    return pl.pallas_call(
        _merge_kernel,
        grid=(t // tm,),
        in_specs=[row(MOBA_WIDTH), row(GLA_WIDTH), row(D_MODEL, COL_GATE_A // D_MODEL), row(D_MODEL, COL_GATE_B // D_MODEL),
                  row(D_MODEL), const(MOBA_WIDTH, D_MODEL), const(GLA_WIDTH, D_MODEL), const(D_MODEL, D_MODEL),
                  const(1, D_MODEL), const(1, D_MODEL)],
        out_specs=[row(D_MODEL), row(D_MODEL)],
        out_shape=[jax.ShapeDtypeStruct((t, D_MODEL), jnp.float32), jax.ShapeDtypeStruct((t, D_MODEL), jnp.bfloat16)],
        compiler_params=_params(("parallel",)),
        name="merge_out_ln",
    )(ya, yb, proj, proj, x, wa, wb, wo, g, b)


def _ffn_kernel(xb_ref, x_ref, wg_ref, wu_ref, wd_ref, g_ref, b_ref, xo_ref, xbo_ref, acc_ref):
    f = pl.program_id(1)

    @pl.when(f == 0)
    def _():
        acc_ref[...] = jnp.zeros_like(acc_ref)

    h = xb_ref[...]
    gate = jnp.dot(h, wg_ref[...], preferred_element_type=jnp.float32)
    up = jnp.dot(h, wu_ref[...], preferred_element_type=jnp.float32)
    hid = gate * jax.nn.sigmoid(gate) * up
    acc_ref[...] += jnp.dot(hid.astype(jnp.bfloat16), wd_ref[...], preferred_element_type=jnp.float32)

    @pl.when(f == pl.num_programs(1) - 1)
    def _():
        y = _layer_norm(DEEPNORM_ALPHA * x_ref[...] + acc_ref[...], g_ref[...], b_ref[...])
        xo_ref[...] = y
        xbo_ref[...] = y.astype(jnp.bfloat16)


def ffn(xb, x, wg, wu, wd, g, b, tm, tf):
    t = x.shape[0]
    fdim = wg.shape[1]
    row = lambda w: pl.BlockSpec((tm, w), lambda i, f: (i, 0))
    const = pl.BlockSpec((1, D_MODEL), lambda i, f: (0, 0))
    return pl.pallas_call(
        _ffn_kernel,
        grid=(t // tm, fdim // tf),
        in_specs=[row(D_MODEL), row(D_MODEL),
                  pl.BlockSpec((D_MODEL, tf), lambda i, f: (0, f)),
                  pl.BlockSpec((D_MODEL, tf), lambda i, f: (0, f)),
                  pl.BlockSpec((tf, D_MODEL), lambda i, f: (f, 0)),
                  const, const],
        out_specs=[row(D_MODEL), row(D_MODEL)],
        out_shape=[jax.ShapeDtypeStruct((t, D_MODEL), jnp.float32), jax.ShapeDtypeStruct((t, D_MODEL), jnp.bfloat16)],
        scratch_shapes=[pltpu.VMEM((tm, D_MODEL), jnp.float32)],
        compiler_params=_params(("parallel", "arbitrary")),
        name="swiglu_ln",
    )(xb, x, wg, wu, wd, g, b)


ROUTE_E1, ROUTE_E2, ROUTE_W1, ROUTE_W2 = 0, 1, 2, 3


def _router_kernel(x_ref, w_ref, o_ref):
    logits = jnp.dot(x_ref[...], w_ref[...], precision=lax.Precision.HIGHEST, preferred_element_type=jnp.float32)
    lane = lax.broadcasted_iota(jnp.int32, (1, LANES), 1)
    g = jnp.where(lane < N_EXPERTS, logits, -jnp.inf)
    m1 = jnp.max(g, axis=-1, keepdims=True)
    i1 = jnp.min(jnp.where(g == m1, lane, LANES), axis=-1, keepdims=True)
    g2 = jnp.where(lane == i1, -jnp.inf, g)
    m2 = jnp.max(g2, axis=-1, keepdims=True)
    i2 = jnp.min(jnp.where(g2 == m2, lane, LANES), axis=-1, keepdims=True)
    e2 = jnp.exp(m2 - m1)
    w1 = 1.0 / (1.0 + e2)
    w2 = e2 / (1.0 + e2)
    out = jnp.where(lane == ROUTE_E1, i1.astype(jnp.float32), 0.0)
    out = jnp.where(lane == ROUTE_E2, i2.astype(jnp.float32), out)
    out = jnp.where(lane == ROUTE_W1, w1, out)
    o_ref[...] = jnp.where(lane == ROUTE_W2, w2, out)


def router(x, w, tm):
    t = x.shape[0]
    return pl.pallas_call(
        _router_kernel,
        grid=(t // tm,),
        in_specs=[pl.BlockSpec((tm, D_MODEL), lambda i: (i, 0)), pl.BlockSpec((D_MODEL, LANES), lambda i: (0, 0))],
        out_specs=pl.BlockSpec((tm, LANES), lambda i: (i, 0)),
        out_shape=jax.ShapeDtypeStruct((t, LANES), jnp.float32),
        compiler_params=_params(("parallel",)),
        name="router",
    )(x, w)


def _routing_tables(route, tm):
    t = route.shape[0]
    experts = route[:, ROUTE_E1:ROUTE_E2 + 1].astype(jnp.int32).reshape(-1)
    onehot = (experts[:, None] == jnp.arange(N_EXPERTS, dtype=jnp.int32)[None, :]).astype(jnp.int32)
    csum = jnp.cumsum(onehot, axis=0)
    rank = jnp.sum((csum - onehot) * onehot, axis=1)
    size = -(-csum[-1] // tm) * tm
    end = jnp.cumsum(size)
    pos = jnp.sum(onehot * (end - size)[None, :], axis=1) + rank
    n_tiles = 2 * t // tm + N_EXPERTS
    tok = jnp.zeros(((n_tiles + 1) * tm,), jnp.int32).at[pos].set(jnp.arange(2 * t, dtype=jnp.int32) // 2)
    tile_start = jnp.arange(n_tiles, dtype=jnp.int32) * tm
    tile_expert = jnp.minimum(jnp.sum((tile_start[:, None] >= end[None, :]).astype(jnp.int32), axis=1), N_EXPERTS - 1)
    n_used = end[-1] // tm
    return tok.reshape(n_tiles + 1, tm), pos.reshape(t, 2), tile_expert, n_used.reshape(1)


SUBLANES = 8


def _issue_row_copies(slot, n_rows, idx_ref, src_hbm, dst_ref, sem):
    for s in range(2):
        @pl.when(slot == s)
        def _():
            def body(j, c):
                base = pl.multiple_of(j * SUBLANES, SUBLANES)
                for u in range(SUBLANES):
                    pltpu.make_async_copy(src_hbm.at[pl.ds(idx_ref[s, base + u], 1), :],
                                          dst_ref.at[s, pl.ds(base + u, 1), :], sem.at[s]).start()
                return c
            lax.fori_loop(0, n_rows // SUBLANES, body, 0)


def _moe_kernel(te_ref, nu_ref, tok_hbm, x_hbm, wg_ref, wu_ref, wd_ref, y_ref,
                idx_ref, xg_ref, xb_ref, acc_ref, idx_sem, row_sem, *, tm):
    i = pl.program_id(0)
    f = pl.program_id(1)
    nt = pl.num_programs(0)
    nf = pl.num_programs(1)
    slot = i % 2
    nxt = 1 - slot
    chunk = tm // (nf + 1)

    def idx_copy(tile, s):
        return pltpu.make_async_copy(tok_hbm.at[tile], idx_ref.at[s], idx_sem.at[s])

    def row_copy(s, r):
        return pltpu.make_async_copy(x_hbm.at[pl.ds(idx_ref[s, r], 1), :], xg_ref.at[s, pl.ds(r, 1), :], row_sem.at[s])

    def rows_done(s):
        return pltpu.make_async_copy(x_hbm.at[pl.ds(0, tm), :], xg_ref.at[s], row_sem.at[s])

    def issue_chunk():
        first = (f + 1) * chunk
        for u in range(chunk):
            row_copy(nxt, first + u).start()

    @pl.when(f == 0)
    def _():
        @pl.when(i == 0)
        def _():
            idx_copy(0, 0).start()
            idx_copy(0, 0).wait()
            _issue_row_copies(0, tm, idx_ref, x_hbm, xg_ref, row_sem)
            idx_copy(1, 1).start()

        idx_copy(i + 1, nxt).wait()
        _issue_row_copies(nxt, chunk, idx_ref, x_hbm, xg_ref, row_sem)
        rows_done(slot).wait()
        xb_ref[...] = xg_ref[slot].astype(jnp.bfloat16)
        acc_ref[...] = jnp.zeros_like(acc_ref)

    @pl.when((f == 1) & (i + 1 < nt))
    def _():
        idx_copy(i + 2, slot).start()

    valid = i < nu_ref[0]

    @pl.when(valid)
    def _():
        issue_chunk()
        h = xb_ref[...]
        gate = jnp.dot(h, wg_ref[0], preferred_element_type=jnp.float32)
        up = jnp.dot(h, wu_ref[0], preferred_element_type=jnp.float32)
        hid = gate * jax.nn.sigmoid(gate) * up
        acc_ref[...] += jnp.dot(hid.astype(jnp.bfloat16), wd_ref[0], preferred_element_type=jnp.float32)

    @pl.when(jnp.logical_not(valid))
    def _():
        issue_chunk()

    @pl.when(f == nf - 1)
    def _():
        y_ref[...] = acc_ref[...]

        @pl.when(i == nt - 1)
        def _():
            rows_done(nxt).wait()


def moe_experts(tile_expert, n_used, tok, x, wg, wu, wd, tm, tf):
    n_tiles = tok.shape[0] - 1
    fdim = wg.shape[2]
    assert fdim // tf >= 2 and n_tiles >= 2 and tm % (SUBLANES * (fdim // tf + 1)) == 0
    grid_spec = pltpu.PrefetchScalarGridSpec(
        num_scalar_prefetch=2,
        grid=(n_tiles, fdim // tf),
        in_specs=[pl.BlockSpec(memory_space=pl.ANY), pl.BlockSpec(memory_space=pl.ANY),
                  pl.BlockSpec((1, D_MODEL, tf), lambda i, f, te, nu: (te[i], 0, f)),
                  pl.BlockSpec((1, D_MODEL, tf), lambda i, f, te, nu: (te[i], 0, f)),
                  pl.BlockSpec((1, tf, D_MODEL), lambda i, f, te, nu: (te[i], f, 0))],
        out_specs=pl.BlockSpec((tm, D_MODEL), lambda i, f, te, nu: (i, 0)),
        scratch_shapes=[pltpu.SMEM((2, tm), jnp.int32), pltpu.VMEM((2, tm, D_MODEL), jnp.float32),
                        pltpu.VMEM((tm, D_MODEL), jnp.bfloat16), pltpu.VMEM((tm, D_MODEL), jnp.float32),
                        pltpu.SemaphoreType.DMA((2,)), pltpu.SemaphoreType.DMA((2,))])
    return pl.pallas_call(
        functools.partial(_moe_kernel, tm=tm),
        grid_spec=grid_spec,
        out_shape=jax.ShapeDtypeStruct((n_tiles * tm, D_MODEL), jnp.float32),
        compiler_params=_params(("arbitrary", "arbitrary"), disable_bounds_checks=True),
        name="moe_experts",
    )(tile_expert, n_used, tok, x, wg, wu, wd)


def _combine_kernel(pos_hbm, y_hbm, route_ref, x_ref, g_ref, b_ref, xo_ref, idx_ref, rows_ref, idx_sem, row_sem, *, tc):
    i = pl.program_id(0)
    nt = pl.num_programs(0)
    slot = i % 2

    def idx_copy(tile, s):
        return pltpu.make_async_copy(pos_hbm.at[tile], idx_ref.at[s], idx_sem.at[s])

    def gather_rows(s):
        _issue_row_copies(s, 2 * tc, idx_ref, y_hbm, rows_ref, row_sem)

    @pl.when(i == 0)
    def _():
        idx_copy(0, 0).start()
        idx_copy(0, 0).wait()
        gather_rows(0)
        idx_copy(1, 1).start()

    @pl.when(i + 1 < nt)
    def _():
        idx_copy(i + 1, 1 - slot).wait()
        gather_rows(1 - slot)

    pltpu.make_async_copy(y_hbm.at[pl.ds(0, 2 * tc), :], rows_ref.at[slot], row_sem.at[slot]).wait()

    @pl.when(i + 2 < nt)
    def _():
        idx_copy(i + 2, slot).start()

    route = route_ref[...]
    w1 = route[:, ROUTE_W1:ROUTE_W1 + 1]
    w2 = route[:, ROUTE_W2:ROUTE_W2 + 1]
    f = w1 * rows_ref[slot, :tc, :] + w2 * rows_ref[slot, tc:, :]
    xo_ref[...] = _layer_norm(DEEPNORM_ALPHA * x_ref[...] + f, g_ref[...], b_ref[...])


def moe_combine(pos, y, route, x, g, b, tc):
    t = x.shape[0]
    n_tiles = t // tc
    assert n_tiles >= 2
    pos_tiles = pos.reshape(n_tiles, tc, 2).transpose(0, 2, 1).reshape(n_tiles, 2 * tc)
    const = pl.BlockSpec((1, D_MODEL), lambda i: (0, 0))
    return pl.pallas_call(
        functools.partial(_combine_kernel, tc=tc),
        grid=(n_tiles,),
        in_specs=[pl.BlockSpec(memory_space=pl.ANY), pl.BlockSpec(memory_space=pl.ANY),
                  pl.BlockSpec((tc, LANES), lambda i: (i, 0)), pl.BlockSpec((tc, D_MODEL), lambda i: (i, 0)),
                  const, const],
        out_specs=pl.BlockSpec((tc, D_MODEL), lambda i: (i, 0)),
        out_shape=jax.ShapeDtypeStruct((t, D_MODEL), jnp.float32),
        scratch_shapes=[pltpu.SMEM((2, 2 * tc), jnp.int32), pltpu.VMEM((2, 2 * tc, D_MODEL), jnp.float32),
                        pltpu.SemaphoreType.DMA((2,)), pltpu.SemaphoreType.DMA((2,))],
        compiler_params=_params(("arbitrary",), disable_bounds_checks=True),
        name="moe_combine_ln",
    )(pos_tiles, y, route, x, g, b)


def _rope_tables(seq):
    half = ROPE_DIM // 2
    inv_freq = jnp.power(ROPE_THETA, -jnp.arange(0, ROPE_DIM, 2, dtype=jnp.float32) / ROPE_DIM)
    ang = jnp.arange(seq, dtype=jnp.float32)[:, None] * inv_freq[None, :]
    cos, sin = jnp.cos(ang), jnp.sin(ang)
    ones = jnp.ones((seq, MOBA_HEAD_DIM - ROPE_DIM), jnp.float32)
    zeros = jnp.zeros((seq, MOBA_HEAD_DIM - ROPE_DIM), jnp.float32)
    z8 = jnp.zeros((seq, half), jnp.float32)
    tab_c = jnp.concatenate([cos, cos, ones], axis=1)
    tab_a = jnp.concatenate([-sin, z8, zeros], axis=1)
    tab_b = jnp.concatenate([z8, sin, zeros], axis=1)
    dup = lambda t: jnp.concatenate([t, t], axis=1)
    return dup(tab_c), dup(tab_a), dup(tab_b)


def _split_w_in(w):
    offs = np.cumsum([0, 512, 512, 512, 512, 512, 512, 512, GLA_GATE_RANK, 1024, 1024])
    part = lambda j: w[:, int(offs[j]):int(offs[j + 1])]
    qa, ka, va, qg, kg, vg, rg, lg, gate_a, gate_b = [part(j) for j in range(10)]
    main = jnp.concatenate([gate_a, gate_b, qa, ka, va, qg, kg, vg, rg], axis=1).astype(jnp.bfloat16)
    lg = jnp.pad(lg, ((0, 0), (0, LANES - GLA_GATE_RANK))).astype(jnp.bfloat16)
    return main, lg


def kernel(x, w_in, w_gla_gate_up, b_gla_gate, gla_norm_g, w_branch_a, w_branch_b, w_out, ln_mix_g, ln_mix_b,
           ffn_w_gate, ffn_w_up, ffn_w_down, moe_w_router, moe_w_gate, moe_w_up, moe_w_down, ln_ffn_g, ln_ffn_b):
    batch, seq, d = x.shape
    t = batch * seq
    bf16_rows = 16
    nb_pad = -(-(seq // MOBA_BLOCK) // bf16_rows) * bf16_rows
    assert d == D_MODEL and seq % MOBA_BLOCK == 0 and nb_pad <= LANES - MOBA_HEAD_DIM
    bf = jnp.bfloat16
    tab_c, tab_a, tab_b = _rope_tables(seq)
    xf = x.reshape(t, d)
    xb = xf.astype(bf)
    row_tile = math.gcd(t, 1024)
    for layer in range(DEPTH):
        w_main, w_lg = _split_w_in(w_in[layer])
        proj = matmul(xb, w_main, bf, row_tile, 512)
        qt, kp, vt, kmean = rope_prepass(proj, tab_c, tab_a, tab_b, batch, seq)
        km = jnp.pad(kmean.transpose(0, 2, 1, 3), ((0, 0), (0, 0), (0, nb_pad - seq // MOBA_BLOCK), (0, 0)))
        y_a = moba_attention(qt, kp, vt, km, batch, seq)
        w_up = jnp.pad(w_gla_gate_up[layer], ((0, LANES - GLA_GATE_RANK), (0, 0)))
        g = gla_gate(xb, w_lg, w_up, b_gla_gate[layer].reshape(1, -1), math.gcd(t, 512))
        y_b = gla(proj, g, gla_norm_g[layer].reshape(1, -1), batch, seq, 512)
        xf, xb = merge_out(y_a, y_b, proj, xf, w_branch_a[layer].astype(bf), w_branch_b[layer].astype(bf),
                           w_out[layer].astype(bf), ln_mix_g[layer].reshape(1, -1), ln_mix_b[layer].reshape(1, -1),
                           math.gcd(t, 512))
        j = layer // 2
        ln_g, ln_b = ln_ffn_g[layer].reshape(1, -1), ln_ffn_b[layer].reshape(1, -1)
        if layer % 2 == 0:
            xf, xb = ffn(xb, xf, ffn_w_gate[j].astype(bf), ffn_w_up[j].astype(bf), ffn_w_down[j].astype(bf),
                         ln_g, ln_b, row_tile, 512)
        else:
            w_r = jnp.pad(moe_w_router[j], ((0, 0), (0, LANES - N_EXPERTS)))
            route = router(xf, w_r, math.gcd(t, 512))
            tok, pos, tile_expert, n_used = _routing_tables(route, row_tile)
            y = moe_experts(tile_expert, n_used, tok, xf, moe_w_gate[j].astype(bf), moe_w_up[j].astype(bf),
                            moe_w_down[j].astype(bf), row_tile, 512)
            xf = moe_combine(pos, y, route, xf, ln_g, ln_b, math.gcd(t, 512))
    return xf.reshape(batch, seq, d)
```

```python
import functools
import math

import jax
import jax.numpy as jnp
import numpy as np
from jax import lax
from jax.experimental import pallas as pl
from jax.experimental.pallas import tpu as pltpu

D_MODEL = 1024
DEPTH = 2
MOBA_HEADS = 8
MOBA_HEAD_DIM = 64
MOBA_WIDTH = 512
MOBA_BLOCK = 256
MOBA_TOP_BLOCKS = 3
ROPE_THETA = 500000.0
ROPE_DIM = 16
GLA_HEADS = 4
GLA_KEY_DIM = 128
GLA_WIDTH = 512
GLA_GATE_RANK = 16
GLA_GATE_TAU = 16.0
GLA_CHUNK = 64
FFN_DIM = 3584
N_EXPERTS = 8
DEEPNORM_ALPHA = (2 * DEPTH) ** 0.25
LN_EPS = 1e-5
MASK_VALUE = -1e30

LANES = 128
VMEM_LIMIT = 56 * 1024 * 1024

COL_GATE_A, COL_GATE_B = 0, 1024
COL_QA, COL_KA, COL_VA = 2048, 2560, 3072
COL_QG, COL_KG, COL_VG, COL_RG = 3584, 4096, 4608, 5120
MAIN_COLS = 5632


def _params(sem, **kw):
    return pltpu.CompilerParams(dimension_semantics=sem, vmem_limit_bytes=VMEM_LIMIT, **kw)


CUM_BLOCK = 256


def _split_bf16(a):
    f32, bf = jnp.float32, jnp.bfloat16
    hi = a.astype(bf)
    r = a - hi.astype(f32)
    mid = r.astype(bf)
    return hi, mid, (r - mid.astype(f32)).astype(bf)


def _in_proj_kernel(x_ref, w_ref, wlg_ref, wup_ref, b_ref, o_ref, cum_ref):
    x = x_ref[...]
    o_ref[...] = jnp.dot(x, w_ref[...], preferred_element_type=jnp.float32).astype(o_ref.dtype)

    @pl.when(pl.program_id(1) == 0)
    def _():
        f32 = jnp.float32
        dot = functools.partial(jnp.dot, preferred_element_type=f32)
        lg_hi, lg_mid, _ = _split_bf16(dot(x, wlg_ref[...]))
        w_hi, w_mid, _ = _split_bf16(wup_ref[...])
        z = dot(lg_hi, w_hi) + (dot(lg_hi, w_mid) + dot(lg_mid, w_hi)) + b_ref[...]
        g = (jnp.minimum(z, 0.0) - jnp.log(1.0 + jnp.exp(-jnp.abs(z)))) * (1.0 / GLA_GATE_TAU)
        r_i = lax.broadcasted_iota(jnp.int32, (CUM_BLOCK, CUM_BLOCK), 0)
        c_i = lax.broadcasted_iota(jnp.int32, (CUM_BLOCK, CUM_BLOCK), 1)
        tri = jnp.where((r_i >= c_i) & (r_i // GLA_CHUNK == c_i // GLA_CHUNK), 1.0, 0.0).astype(jnp.bfloat16)
        for c in range(x.shape[0] // CUM_BLOCK):
            sl = slice(c * CUM_BLOCK, (c + 1) * CUM_BLOCK)
            g_hi, g_mid, g_lo = _split_bf16(g[sl, :])
            cum_ref[sl, :] = dot(tri, g_hi) + (dot(tri, g_mid) + dot(tri, g_lo))


def in_proj(x, w, w_lg, w_up, b, tm, tn):
    m, k = x.shape
    n = w.shape[1]
    assert m % tm == 0 and n % tn == 0 and tm % CUM_BLOCK == 0 and CUM_BLOCK % GLA_CHUNK == 0
    const = lambda r, c: pl.BlockSpec((r, c), lambda i, j: (0, 0))
    return pl.pallas_call(
        _in_proj_kernel,
        grid=(m // tm, n // tn),
        in_specs=[pl.BlockSpec((tm, k), lambda i, j: (i, 0)),
                  pl.BlockSpec((k, tn), lambda i, j: (0, j)),
                  const(k, LANES), const(LANES, GLA_WIDTH), const(1, GLA_WIDTH)],
        out_specs=[pl.BlockSpec((tm, tn), lambda i, j: (i, j)),
                   pl.BlockSpec((tm, GLA_WIDTH), lambda i, j: (i, 0))],
        out_shape=[jax.ShapeDtypeStruct((m, n), jnp.bfloat16), jax.ShapeDtypeStruct((m, GLA_WIDTH), jnp.float32)],
        compiler_params=_params(("parallel", "arbitrary")),
        name="in_proj",
    )(x, w, w_lg, w_up, b)


def _rope_kernel(q_ref, k_ref, v_ref, c_ref, a_ref, b_ref, qt_ref, kp_ref, vt_ref, km_ref, *, per_seq):
    half = MOBA_HEAD_DIM
    blk = pl.program_id(0) % per_seq
    lane = lax.broadcasted_iota(jnp.int32, (1, LANES), 1)
    lower = lane < half
    ind = jnp.where(lane == half + blk, 1.0, 0.0)
    zeros = jnp.zeros((LANES - half, MOBA_BLOCK), jnp.float32)
    ones = jnp.ones((LANES - half, MOBA_BLOCK), jnp.float32)

    def rot(t):
        return (t * c_ref[...] + pltpu.roll(t, LANES - ROPE_DIM // 2, 1) * a_ref[...]
                + pltpu.roll(t, ROPE_DIM // 2, 1) * b_ref[...])

    for hp in range(MOBA_WIDTH // LANES):
        sl = slice(hp * LANES, (hp + 1) * LANES)
        f32 = jnp.float32
        q_t = (rot(q_ref[:, sl].astype(f32)) * (MOBA_HEAD_DIM ** -0.5)).T
        v_t = v_ref[:, sl].astype(f32).T
        kr = rot(k_ref[:, sl].astype(f32))
        km = jnp.sum(kr, axis=0, keepdims=True) * (1.0 / MOBA_BLOCK)
        for j, (kh, kmh) in enumerate(((kr, km), (pltpu.roll(kr, half, 1), pltpu.roll(km, half, 1)))):
            h = 2 * hp + j
            rows = slice(j * half, (j + 1) * half)
            qt_ref[0, h] = jnp.concatenate([q_t[rows], zeros], axis=0).astype(qt_ref.dtype)
            vt_ref[0, h, 0] = jnp.concatenate([v_t[rows], ones], axis=0).astype(vt_ref.dtype)
            kp_ref[0, h, 0] = jnp.where(lower, kh, ind).astype(kp_ref.dtype)
            km_ref[0, 0, h:h + 1, :] = jnp.where(lower, kmh, 0.0)


def rope_prepass(proj, tab_c, tab_a, tab_b, batch, seq):
    per_seq = seq // MOBA_BLOCK
    row = lambda cb: pl.BlockSpec((MOBA_BLOCK, MOBA_WIDTH), lambda i: (i, cb))
    tab = pl.BlockSpec((MOBA_BLOCK, LANES), lambda i: (i % per_seq, 0))
    bf = jnp.bfloat16
    return pl.pallas_call(
        functools.partial(_rope_kernel, per_seq=per_seq),
        grid=(batch * per_seq,),
        in_specs=[row(COL_QA // MOBA_WIDTH), row(COL_KA // MOBA_WIDTH), row(COL_VA // MOBA_WIDTH), tab, tab, tab],
        out_specs=[pl.BlockSpec((1, MOBA_HEADS, LANES, MOBA_BLOCK), lambda i: (i // per_seq, 0, 0, i % per_seq)),
                   pl.BlockSpec((1, MOBA_HEADS, 1, MOBA_BLOCK, LANES), lambda i: (i // per_seq, 0, i % per_seq, 0, 0)),
                   pl.BlockSpec((1, MOBA_HEADS, 1, LANES, MOBA_BLOCK), lambda i: (i // per_seq, 0, i % per_seq, 0, 0)),
                   pl.BlockSpec((1, 1, MOBA_HEADS, LANES), lambda i: (i // per_seq, i % per_seq, 0, 0))],
        out_shape=[jax.ShapeDtypeStruct((batch, MOBA_HEADS, LANES, seq), bf),
                   jax.ShapeDtypeStruct((batch, MOBA_HEADS, per_seq, MOBA_BLOCK, LANES), bf),
                   jax.ShapeDtypeStruct((batch, MOBA_HEADS, per_seq, LANES, MOBA_BLOCK), bf),
                   jax.ShapeDtypeStruct((batch, per_seq, MOBA_HEADS, LANES), jnp.float32)],
        compiler_params=_params(("parallel",)),
        name="rope_kmean",
    )(proj, proj, proj, tab_c, tab_a, tab_b)


HEADS_PER_STEP = 8


def _moba_kernel(qt_ref, kp_ref, vt_ref, km_ref, o_ref, qb_ref, *, nb_pad):
    own = pl.program_id(2)
    half = MOBA_HEAD_DIM
    tq = MOBA_BLOCK
    bf = jnp.bfloat16
    blk = lax.broadcasted_iota(jnp.int32, (nb_pad, tq), 0)
    causal = lax.broadcasted_iota(jnp.int32, (tq, tq), 0) <= lax.broadcasted_iota(jnp.int32, (tq, tq), 1)

    def weights(h, j, m):
        kp = kp_ref[0, h, pl.ds(2 * j, 2)].reshape(2 * MOBA_BLOCK, LANES)
        s = jnp.dot(kp, qb_ref[h], preferred_element_type=jnp.float32)
        m_new = jnp.maximum(m, jnp.max(s, axis=0, keepdims=True))
        return m_new, jnp.exp(s - m_new).astype(bf), jnp.exp(m - m_new)

    def absorb(h, j, acc, p, alpha):
        return (alpha * acc
                + jnp.dot(vt_ref[0, h, 2 * j], p[:MOBA_BLOCK], preferred_element_type=jnp.float32)
                + jnp.dot(vt_ref[0, h, 2 * j + 1], p[MOBA_BLOCK:], preferred_element_type=jnp.float32))

    carry = []
    for h in range(HEADS_PER_STEP):
        qt = qt_ref[0, h]
        gate = jnp.dot(km_ref[0, h], qt.astype(jnp.float32), precision=lax.Precision.HIGHEST,
                       preferred_element_type=jnp.float32)
        g = jnp.where(blk < own, gate, -jnp.inf)
        sel = jnp.zeros(g.shape, jnp.float32)
        for _ in range(MOBA_TOP_BLOCKS):
            m = jnp.max(g, axis=0, keepdims=True)
            idx = jnp.min(jnp.where(g == m, blk, nb_pad), axis=0, keepdims=True)
            hit = blk == idx
            sel = jnp.where(hit & (m > -jnp.inf), 1.0, sel)
            g = jnp.where(hit, -jnp.inf, g)
        bias = jnp.where(sel > 0.0, 0.0, MASK_VALUE).astype(bf)
        qb_ref[h] = jnp.concatenate([qt[:half], bias, jnp.zeros((LANES - half - nb_pad, tq), bf)], axis=0)

        s = jnp.dot(kp_ref[0, h, own], qt, preferred_element_type=jnp.float32)
        s = jnp.where(causal, s, MASK_VALUE)
        m = jnp.max(s, axis=0, keepdims=True)
        p = jnp.exp(s - m)
        acc = jnp.dot(vt_ref[0, h, own], p.astype(bf), preferred_element_type=jnp.float32)
        carry += [acc, *weights(h, 0, m)]

    def step(j, carry):
        new = []
        for h in range(HEADS_PER_STEP):
            acc, m, p, alpha = carry[4 * h:4 * h + 4]
            new += [absorb(h, j - 1, acc, p, alpha), *weights(h, j, m)]
        return tuple(new)

    n_pairs = jnp.maximum((own + 1) // 2, 1)
    carry = lax.fori_loop(1, n_pairs, step, tuple(carry))
    outs = []
    for h in range(HEADS_PER_STEP):
        acc, _, p, alpha = carry[4 * h:4 * h + 4]
        acc = absorb(h, n_pairs - 1, acc, p, alpha)
        outs.append(acc[:half] / acc[half:half + 1])
    o_ref[...] = jnp.concatenate(outs, axis=0).T.astype(o_ref.dtype)


def moba_attention(qt, kp, vt, km, batch, seq):
    nq = seq // MOBA_BLOCK
    nb_pad = km.shape[2]
    hs = HEADS_PER_STEP
    return pl.pallas_call(
        functools.partial(_moba_kernel, nb_pad=nb_pad),
        grid=(batch, MOBA_HEADS // hs, nq),
        in_specs=[pl.BlockSpec((1, hs, LANES, MOBA_BLOCK), lambda b, hp, i: (b, hp, 0, i)),
                  pl.BlockSpec((1, hs, nq, MOBA_BLOCK, LANES), lambda b, hp, i: (b, hp, 0, 0, 0)),
                  pl.BlockSpec((1, hs, nq, LANES, MOBA_BLOCK), lambda b, hp, i: (b, hp, 0, 0, 0)),
                  pl.BlockSpec((1, hs, nb_pad, LANES), lambda b, hp, i: (b, hp, 0, 0))],
        out_specs=pl.BlockSpec((MOBA_BLOCK, hs * MOBA_HEAD_DIM), lambda b, hp, i: (b * nq + i, hp)),
        out_shape=jax.ShapeDtypeStruct((batch * seq, MOBA_WIDTH), jnp.bfloat16),
        scratch_shapes=[pltpu.VMEM((hs, LANES, MOBA_BLOCK), jnp.bfloat16)],
        compiler_params=_params(("parallel", "parallel", "arbitrary")),
        name="moba_attention",
    )(qt, kp, vt, km)


def _gla_kernel(q_ref, k_ref, v_ref, r_ref, cum_ref, gain_ref, o_ref, state_ref, *, rows):
    @pl.when(pl.program_id(1) == 0)
    def _():
        state_ref[...] = jnp.zeros_like(state_ref)

    c = GLA_CHUNK
    r_i = lax.broadcasted_iota(jnp.int32, (c, c), 0)
    c_i = lax.broadcasted_iota(jnp.int32, (c, c), 1)
    lower = r_i >= c_i
    nt = (((1,), (1,)), ((), ()))
    tn = (((0,), (0,)), ((), ()))
    bf = jnp.bfloat16
    f32 = jnp.float32
    for j in range(rows // c):
        sl = slice(j * c, (j + 1) * c)
        for h in range(GLA_HEADS):
            hl = slice(h * LANES, (h + 1) * LANES)
            cum = cum_ref[sl, hl]
            cum_last = cum[c - 1:c, :]
            q = q_ref[sl, hl].astype(f32) * (GLA_KEY_DIM ** -0.5)
            k = k_ref[sl, hl].astype(f32)
            v = v_ref[sl, hl]
            q_dec = (q * jnp.exp(cum)).astype(bf)
            k_inv = (k * jnp.exp(-cum)).astype(bf)
            k_end = (k * jnp.exp(cum_last - cum)).astype(bf)
            att = lax.dot_general(q_dec, k_inv, nt, preferred_element_type=f32)
            att = jnp.where(lower, att, 0.0)
            intra = jnp.dot(att.astype(bf), v, preferred_element_type=f32)
            state = state_ref[h]
            inter = lax.dot_general(q_dec, state.astype(bf), nt, preferred_element_type=f32)
            d_state = lax.dot_general(v, k_end, tn, preferred_element_type=f32)
            state_ref[h] = state * jnp.exp(cum_last) + d_state
            o = intra + inter
            mu = jnp.mean(o, axis=-1, keepdims=True)
            var = jnp.mean(jnp.square(o - mu), axis=-1, keepdims=True)
            on = (o - mu) * lax.rsqrt(var + LN_EPS) * gain_ref[:, hl]
            rg = r_ref[sl, hl].astype(f32)
            o_ref[sl, hl] = (on * (rg * jax.nn.sigmoid(rg))).astype(o_ref.dtype)


def gla(proj, cum, gain, batch, seq, rows):
    t = proj.shape[0]
    assert seq % rows == 0 and rows % GLA_CHUNK == 0
    ns = seq // rows
    col = lambda c0: pl.BlockSpec((rows, GLA_WIDTH), lambda b, i: (b * ns + i, c0 // GLA_WIDTH))
    return pl.pallas_call(
        functools.partial(_gla_kernel, rows=rows),
        grid=(batch, ns),
        in_specs=[col(COL_QG), col(COL_KG), col(COL_VG), col(COL_RG), col(0),
                  pl.BlockSpec((1, GLA_WIDTH), lambda b, i: (0, 0))],
        out_specs=col(0),
        out_shape=jax.ShapeDtypeStruct((t, GLA_WIDTH), jnp.bfloat16),
        scratch_shapes=[pltpu.VMEM((GLA_HEADS, LANES, LANES), jnp.float32)],
        compiler_params=_params(("parallel", "arbitrary")),
        name="gla",
    )(proj, proj, proj, proj, cum, gain)


def _layer_norm(z, g, b):
    mu = jnp.mean(z, axis=-1, keepdims=True)
    var = jnp.mean(jnp.square(z - mu), axis=-1, keepdims=True)
    return (z - mu) * lax.rsqrt(var + LN_EPS) * g + b


def _merge_kernel(ya_ref, yb_ref, ga_ref, gb_ref, x_ref, wa_ref, wb_ref, wo_ref, g_ref, b_ref, xo_ref, xb_ref):
    bf = jnp.bfloat16
    f32 = jnp.float32
    pa = jnp.dot(ya_ref[...], wa_ref[...], preferred_element_type=f32)
    pb = jnp.dot(yb_ref[...], wb_ref[...], preferred_element_type=f32)
    merged = jax.nn.sigmoid(ga_ref[...].astype(f32)) * pa + jax.nn.sigmoid(gb_ref[...].astype(f32)) * pb
    mix = jnp.dot(merged.astype(bf), wo_ref[...], preferred_element_type=jnp.float32)
    y = _layer_norm(DEEPNORM_ALPHA * x_ref[...] + mix, g_ref[...], b_ref[...])
    xo_ref[...] = y
    xb_ref[...] = y.astype(bf)


def merge_out(ya, yb, proj, x, wa, wb, wo, g, b, tm):
    t = x.shape[0]
    row = lambda w, cb=0: pl.BlockSpec((tm, w), lambda i: (i, cb))
    const = lambda r, c: pl.BlockSpec((r, c), lambda i: (0, 0))
    return pl.pallas_call(
        _merge_kernel,
        grid=(t // tm,),
        in_specs=[row(MOBA_WIDTH), row(GLA_WIDTH), row(D_MODEL, COL_GATE_A // D_MODEL), row(D_MODEL, COL_GATE_B // D_MODEL),
                  row(D_MODEL), const(MOBA_WIDTH, D_MODEL), const(GLA_WIDTH, D_MODEL), const(D_MODEL, D_MODEL),
                  const(1, D_MODEL), const(1, D_MODEL)],
        out_specs=[row(D_MODEL), row(D_MODEL)],
        out_shape=[jax.ShapeDtypeStruct((t, D_MODEL), jnp.float32), jax.ShapeDtypeStruct((t, D_MODEL), jnp.bfloat16)],
        compiler_params=_params(("parallel",)),
        name="merge_out_ln",
    )(ya, yb, proj, proj, x, wa, wb, wo, g, b)


def _ffn_kernel(xb_ref, x_ref, wg_ref, wu_ref, wd_ref, g_ref, b_ref, xo_ref, xbo_ref, acc_ref):
    f = pl.program_id(1)

    @pl.when(f == 0)
    def _():
        acc_ref[...] = jnp.zeros_like(acc_ref)

    h = xb_ref[...]
    gate = jnp.dot(h, wg_ref[...], preferred_element_type=jnp.float32)
    up = jnp.dot(h, wu_ref[...], preferred_element_type=jnp.float32)
    hid = gate * jax.nn.sigmoid(gate) * up
    acc_ref[...] += jnp.dot(hid.astype(jnp.bfloat16), wd_ref[...], preferred_element_type=jnp.float32)

    @pl.when(f == pl.num_programs(1) - 1)
    def _():
        y = _layer_norm(DEEPNORM_ALPHA * x_ref[...] + acc_ref[...], g_ref[...], b_ref[...])
        xo_ref[...] = y
        xbo_ref[...] = y.astype(jnp.bfloat16)


def ffn(xb, x, wg, wu, wd, g, b, tm, tf):
    t = x.shape[0]
    fdim = wg.shape[1]
    row = lambda w: pl.BlockSpec((tm, w), lambda i, f: (i, 0))
    const = pl.BlockSpec((1, D_MODEL), lambda i, f: (0, 0))
    return pl.pallas_call(
        _ffn_kernel,
        grid=(t // tm, fdim // tf),
        in_specs=[row(D_MODEL), row(D_MODEL),
                  pl.BlockSpec((D_MODEL, tf), lambda i, f: (0, f)),
                  pl.BlockSpec((D_MODEL, tf), lambda i, f: (0, f)),
                  pl.BlockSpec((tf, D_MODEL), lambda i, f: (f, 0)),
                  const, const],
        out_specs=[row(D_MODEL), row(D_MODEL)],
        out_shape=[jax.ShapeDtypeStruct((t, D_MODEL), jnp.float32), jax.ShapeDtypeStruct((t, D_MODEL), jnp.bfloat16)],
        scratch_shapes=[pltpu.VMEM((tm, D_MODEL), jnp.float32)],
        compiler_params=_params(("parallel", "arbitrary")),
        name="swiglu_ln",
    )(xb, x, wg, wu, wd, g, b)


ROUTE_E1, ROUTE_E2, ROUTE_W1, ROUTE_W2 = 0, 1, 2, 3


def _router_kernel(x_ref, w_ref, o_ref):
    logits = jnp.dot(x_ref[...], w_ref[...], precision=lax.Precision.HIGHEST, preferred_element_type=jnp.float32)
    lane = lax.broadcasted_iota(jnp.int32, (1, LANES), 1)
    g = jnp.where(lane < N_EXPERTS, logits, -jnp.inf)
    m1 = jnp.max(g, axis=-1, keepdims=True)
    i1 = jnp.min(jnp.where(g == m1, lane, LANES), axis=-1, keepdims=True)
    g2 = jnp.where(lane == i1, -jnp.inf, g)
    m2 = jnp.max(g2, axis=-1, keepdims=True)
    i2 = jnp.min(jnp.where(g2 == m2, lane, LANES), axis=-1, keepdims=True)
    e2 = jnp.exp(m2 - m1)
    w1 = 1.0 / (1.0 + e2)
    w2 = e2 / (1.0 + e2)
    out = jnp.where(lane == ROUTE_E1, i1.astype(jnp.float32), 0.0)
    out = jnp.where(lane == ROUTE_E2, i2.astype(jnp.float32), out)
    out = jnp.where(lane == ROUTE_W1, w1, out)
    o_ref[...] = jnp.where(lane == ROUTE_W2, w2, out)


def router(x, w, tm):
    t = x.shape[0]
    return pl.pallas_call(
        _router_kernel,
        grid=(t // tm,),
        in_specs=[pl.BlockSpec((tm, D_MODEL), lambda i: (i, 0)), pl.BlockSpec((D_MODEL, LANES), lambda i: (0, 0))],
        out_specs=pl.BlockSpec((tm, LANES), lambda i: (i, 0)),
        out_shape=jax.ShapeDtypeStruct((t, LANES), jnp.float32),
        compiler_params=_params(("parallel",)),
        name="router",
    )(x, w)


def _routing_tables(route, tm):
    t = route.shape[0]
    experts = route[:, ROUTE_E1:ROUTE_E2 + 1].astype(jnp.int32).reshape(-1)
    onehot = (experts[:, None] == jnp.arange(N_EXPERTS, dtype=jnp.int32)[None, :]).astype(jnp.int32)
    csum = jnp.cumsum(onehot, axis=0)
    rank = jnp.sum((csum - onehot) * onehot, axis=1)
    size = -(-csum[-1] // tm) * tm
    end = jnp.cumsum(size)
    pos = jnp.sum(onehot * (end - size)[None, :], axis=1) + rank
    n_tiles = 2 * t // tm + N_EXPERTS
    tok = jnp.zeros(((n_tiles + 1) * tm,), jnp.int32).at[pos].set(
        jnp.arange(2 * t, dtype=jnp.int32) // 2, unique_indices=True, mode="promise_in_bounds")
    tile_start = jnp.arange(n_tiles, dtype=jnp.int32) * tm
    tile_expert = jnp.minimum(jnp.sum((tile_start[:, None] >= end[None, :]).astype(jnp.int32), axis=1), N_EXPERTS - 1)
    n_used = end[-1] // tm
    return tok.reshape(n_tiles + 1, tm), pos.reshape(t, 2), tile_expert, n_used.reshape(1)


SUBLANES = 8


def _issue_row_copies(slot, n_rows, idx_ref, src_hbm, dst_ref, sem):
    for s in range(2):
        @pl.when(slot == s)
        def _():
            def body(j, c):
                base = pl.multiple_of(j * SUBLANES, SUBLANES)
                for u in range(SUBLANES):
                    pltpu.make_async_copy(src_hbm.at[pl.ds(idx_ref[s, base + u], 1), :],
                                          dst_ref.at[s, pl.ds(base + u, 1), :], sem.at[s]).start()
                return c
            lax.fori_loop(0, n_rows // SUBLANES, body, 0)


def _moe_kernel(te_ref, nu_ref, tok_hbm, x_hbm, wg_ref, wu_ref, wd_ref, y_ref,
                idx_ref, xg_ref, xb_ref, acc_ref, idx_sem, row_sem, *, tm):
    i = pl.program_id(0)
    f = pl.program_id(1)
    nt = pl.num_programs(0)
    nf = pl.num_programs(1)
    slot = i % 2
    nxt = 1 - slot
    chunk = tm // (nf + 1)

    def idx_copy(tile, s):
        return pltpu.make_async_copy(tok_hbm.at[tile], idx_ref.at[s], idx_sem.at[s])

    def row_copy(s, r):
        return pltpu.make_async_copy(x_hbm.at[pl.ds(idx_ref[s, r], 1), :], xg_ref.at[s, pl.ds(r, 1), :], row_sem.at[s])

    def rows_done(s):
        return pltpu.make_async_copy(x_hbm.at[pl.ds(0, tm), :], xg_ref.at[s], row_sem.at[s])

    def issue_chunk():
        first = (f + 1) * chunk
        for u in range(chunk):
            row_copy(nxt, first + u).start()

    @pl.when(f == 0)
    def _():
        @pl.when(i == 0)
        def _():
            idx_copy(0, 0).start()
            idx_copy(0, 0).wait()
            _issue_row_copies(0, tm, idx_ref, x_hbm, xg_ref, row_sem)
            idx_copy(1, 1).start()

        idx_copy(i + 1, nxt).wait()
        _issue_row_copies(nxt, chunk, idx_ref, x_hbm, xg_ref, row_sem)
        rows_done(slot).wait()
        xb_ref[...] = xg_ref[slot].astype(jnp.bfloat16)
        acc_ref[...] = jnp.zeros_like(acc_ref)

    @pl.when((f == 1) & (i + 1 < nt))
    def _():
        idx_copy(i + 2, slot).start()

    valid = i < nu_ref[0]

    @pl.when(valid)
    def _():
        issue_chunk()
        h = xb_ref[...]
        gate = jnp.dot(h, wg_ref[0], preferred_element_type=jnp.float32)
        up = jnp.dot(h, wu_ref[0], preferred_element_type=jnp.float32)
        hid = gate * jax.nn.sigmoid(gate) * up
        acc_ref[...] += jnp.dot(hid.astype(jnp.bfloat16), wd_ref[0], preferred_element_type=jnp.float32)

    @pl.when(jnp.logical_not(valid))
    def _():
        issue_chunk()

    @pl.when(f == nf - 1)
    def _():
        y_ref[...] = acc_ref[...]

        @pl.when(i == nt - 1)
        def _():
            rows_done(nxt).wait()


def moe_experts(tile_expert, n_used, tok, x, wg, wu, wd, tm, tf):
    n_tiles = tok.shape[0] - 1
    fdim = wg.shape[2]
    assert fdim // tf >= 2 and n_tiles >= 2 and tm % (SUBLANES * (fdim // tf + 1)) == 0
    grid_spec = pltpu.PrefetchScalarGridSpec(
        num_scalar_prefetch=2,
        grid=(n_tiles, fdim // tf),
        in_specs=[pl.BlockSpec(memory_space=pl.ANY), pl.BlockSpec(memory_space=pl.ANY),
                  pl.BlockSpec((1, D_MODEL, tf), lambda i, f, te, nu: (te[i], 0, f)),
                  pl.BlockSpec((1, D_MODEL, tf), lambda i, f, te, nu: (te[i], 0, f)),
                  pl.BlockSpec((1, tf, D_MODEL), lambda i, f, te, nu: (te[i], f, 0))],
        out_specs=pl.BlockSpec((tm, D_MODEL), lambda i, f, te, nu: (i, 0)),
        scratch_shapes=[pltpu.SMEM((2, tm), jnp.int32), pltpu.VMEM((2, tm, D_MODEL), jnp.float32),
                        pltpu.VMEM((tm, D_MODEL), jnp.bfloat16), pltpu.VMEM((tm, D_MODEL), jnp.float32),
                        pltpu.SemaphoreType.DMA((2,)), pltpu.SemaphoreType.DMA((2,))])
    return pl.pallas_call(
        functools.partial(_moe_kernel, tm=tm),
        grid_spec=grid_spec,
        out_shape=jax.ShapeDtypeStruct((n_tiles * tm, D_MODEL), jnp.float32),
        compiler_params=_params(("arbitrary", "arbitrary"), disable_bounds_checks=True),
        name="moe_experts",
    )(tile_expert, n_used, tok, x, wg, wu, wd)


def _combine_kernel(pos_hbm, y_hbm, route_ref, x_ref, g_ref, b_ref, xo_ref, idx_ref, rows_ref, idx_sem, row_sem, *, tc):
    i = pl.program_id(0)
    nt = pl.num_programs(0)
    slot = i % 2

    def idx_copy(tile, s):
        return pltpu.make_async_copy(pos_hbm.at[tile], idx_ref.at[s], idx_sem.at[s])

    def gather_rows(s):
        _issue_row_copies(s, 2 * tc, idx_ref, y_hbm, rows_ref, row_sem)

    @pl.when(i == 0)
    def _():
        idx_copy(0, 0).start()
        idx_copy(0, 0).wait()
        gather_rows(0)
        idx_copy(1, 1).start()

    @pl.when(i + 1 < nt)
    def _():
        idx_copy(i + 1, 1 - slot).wait()
        gather_rows(1 - slot)

    pltpu.make_async_copy(y_hbm.at[pl.ds(0, 2 * tc), :], rows_ref.at[slot], row_sem.at[slot]).wait()

    @pl.when(i + 2 < nt)
    def _():
        idx_copy(i + 2, slot).start()

    route = route_ref[...]
    w1 = route[:, ROUTE_W1:ROUTE_W1 + 1]
    w2 = route[:, ROUTE_W2:ROUTE_W2 + 1]
    f = w1 * rows_ref[slot, :tc, :] + w2 * rows_ref[slot, tc:, :]
    xo_ref[...] = _layer_norm(DEEPNORM_ALPHA * x_ref[...] + f, g_ref[...], b_ref[...])


def moe_combine(pos, y, route, x, g, b, tc):
    t = x.shape[0]
    n_tiles = t // tc
    assert n_tiles >= 2
    pos_tiles = pos.reshape(n_tiles, tc, 2).transpose(0, 2, 1).reshape(n_tiles, 2 * tc)
    const = pl.BlockSpec((1, D_MODEL), lambda i: (0, 0))
    return pl.pallas_call(
        functools.partial(_combine_kernel, tc=tc),
        grid=(n_tiles,),
        in_specs=[pl.BlockSpec(memory_space=pl.ANY), pl.BlockSpec(memory_space=pl.ANY),
                  pl.BlockSpec((tc, LANES), lambda i: (i, 0)), pl.BlockSpec((tc, D_MODEL), lambda i: (i, 0)),
                  const, const],
        out_specs=pl.BlockSpec((tc, D_MODEL), lambda i: (i, 0)),
        out_shape=jax.ShapeDtypeStruct((t, D_MODEL), jnp.float32),
        scratch_shapes=[pltpu.SMEM((2, 2 * tc), jnp.int32), pltpu.VMEM((2, 2 * tc, D_MODEL), jnp.float32),
                        pltpu.SemaphoreType.DMA((2,)), pltpu.SemaphoreType.DMA((2,))],
        compiler_params=_params(("arbitrary",), disable_bounds_checks=True),
        name="moe_combine_ln",
    )(pos_tiles, y, route, x, g, b)


def _rope_tables(seq):
    half = ROPE_DIM // 2
    inv_freq = jnp.power(ROPE_THETA, -jnp.arange(0, ROPE_DIM, 2, dtype=jnp.float32) / ROPE_DIM)
    ang = jnp.arange(seq, dtype=jnp.float32)[:, None] * inv_freq[None, :]
    cos, sin = jnp.cos(ang), jnp.sin(ang)
    ones = jnp.ones((seq, MOBA_HEAD_DIM - ROPE_DIM), jnp.float32)
    zeros = jnp.zeros((seq, MOBA_HEAD_DIM - ROPE_DIM), jnp.float32)
    z8 = jnp.zeros((seq, half), jnp.float32)
    tab_c = jnp.concatenate([cos, cos, ones], axis=1)
    tab_a = jnp.concatenate([-sin, z8, zeros], axis=1)
    tab_b = jnp.concatenate([z8, sin, zeros], axis=1)
    dup = lambda t: jnp.concatenate([t, t], axis=1)
    return dup(tab_c), dup(tab_a), dup(tab_b)


def _split_w_in(w):
    offs = np.cumsum([0, 512, 512, 512, 512, 512, 512, 512, GLA_GATE_RANK, 1024, 1024])
    part = lambda j: w[:, int(offs[j]):int(offs[j + 1])]
    qa, ka, va, qg, kg, vg, rg, lg, gate_a, gate_b = [part(j) for j in range(10)]
    main = jnp.concatenate([gate_a, gate_b, qa, ka, va, qg, kg, vg, rg], axis=1).astype(jnp.bfloat16)
    lg = jnp.pad(lg, ((0, 0), (0, LANES - GLA_GATE_RANK))).astype(jnp.bfloat16)
    return main, lg


def kernel(x, w_in, w_gla_gate_up, b_gla_gate, gla_norm_g, w_branch_a, w_branch_b, w_out, ln_mix_g, ln_mix_b,
           ffn_w_gate, ffn_w_up, ffn_w_down, moe_w_router, moe_w_gate, moe_w_up, moe_w_down, ln_ffn_g, ln_ffn_b):
    batch, seq, d = x.shape
    t = batch * seq
    bf16_rows = 16
    nb_pad = -(-(seq // MOBA_BLOCK) // bf16_rows) * bf16_rows
    assert d == D_MODEL and seq % MOBA_BLOCK == 0 and nb_pad <= LANES - MOBA_HEAD_DIM
    bf = jnp.bfloat16
    tab_c, tab_a, tab_b = _rope_tables(seq)
    xf = x.reshape(t, d)
    xb = xf.astype(bf)
    row_tile = math.gcd(t, 1024)
    for layer in range(DEPTH):
        w_main, w_lg = _split_w_in(w_in[layer])
        w_up = jnp.pad(w_gla_gate_up[layer], ((0, LANES - GLA_GATE_RANK), (0, 0)))
        proj, cum = in_proj(xb, w_main, w_lg, w_up, b_gla_gate[layer].reshape(1, -1), math.gcd(t, 2048), 512)
        qt, kp, vt, kmean = rope_prepass(proj, tab_c, tab_a, tab_b, batch, seq)
        km = jnp.pad(kmean.transpose(0, 2, 1, 3), ((0, 0), (0, 0), (0, nb_pad - seq // MOBA_BLOCK), (0, 0)))
        y_a = moba_attention(qt, kp, vt, km, batch, seq)
        y_b = gla(proj, cum, gla_norm_g[layer].reshape(1, -1), batch, seq, 512)
        xf, xb = merge_out(y_a, y_b, proj, xf, w_branch_a[layer].astype(bf), w_branch_b[layer].astype(bf),
                           w_out[layer].astype(bf), ln_mix_g[layer].reshape(1, -1), ln_mix_b[layer].reshape(1, -1),
                           math.gcd(t, 512))
        j = layer // 2
        ln_g, ln_b = ln_ffn_g[layer].reshape(1, -1), ln_ffn_b[layer].reshape(1, -1)
        if layer % 2 == 0:
            xf, xb = ffn(xb, xf, ffn_w_gate[j].astype(bf), ffn_w_up[j].astype(bf), ffn_w_down[j].astype(bf),
                         ln_g, ln_b, row_tile, 512)
        else:
            w_r = jnp.pad(moe_w_router[j], ((0, 0), (0, LANES - N_EXPERTS)))
            route = router(xf, w_r, math.gcd(t, 512))
            tok, pos, tile_expert, n_used = _routing_tables(route, row_tile)
            y = moe_experts(tile_expert, n_used, tok, xf, moe_w_gate[j].astype(bf), moe_w_up[j].astype(bf),
                            moe_w_down[j].astype(bf), row_tile, 512)
            xf = moe_combine(pos, y, route, xf, ln_g, ln_b, math.gcd(t, 512))
    return xf.reshape(batch, seq, d)
```

```python
import functools
import math

import jax
import jax.numpy as jnp
import numpy as np
from jax import lax
from jax.experimental import pallas as pl
from jax.experimental.pallas import tpu as pltpu

D_MODEL = 1024
DEPTH = 2
MOBA_HEADS = 8
MOBA_HEAD_DIM = 64
MOBA_WIDTH = 512
MOBA_BLOCK = 256
MOBA_TOP_BLOCKS = 3
ROPE_THETA = 500000.0
ROPE_DIM = 16
GLA_HEADS = 4
GLA_KEY_DIM = 128
GLA_WIDTH = 512
GLA_GATE_RANK = 16
GLA_GATE_TAU = 16.0
GLA_CHUNK = 64
FFN_DIM = 3584
N_EXPERTS = 8
DEEPNORM_ALPHA = (2 * DEPTH) ** 0.25
LN_EPS = 1e-5
MASK_VALUE = -1e30

LANES = 128
VMEM_LIMIT = 56 * 1024 * 1024

COL_GATE_A, COL_GATE_B = 0, 1024
COL_QA, COL_KA, COL_VA = 2048, 2560, 3072
COL_QG, COL_KG, COL_VG, COL_RG = 3584, 4096, 4608, 5120
MAIN_COLS = 5632


def _params(sem, **kw):
    return pltpu.CompilerParams(dimension_semantics=sem, vmem_limit_bytes=VMEM_LIMIT, **kw)


CUM_BLOCK = 256


def _split_bf16(a):
    f32, bf = jnp.float32, jnp.bfloat16
    hi = a.astype(bf)
    r = a - hi.astype(f32)
    mid = r.astype(bf)
    return hi, mid, (r - mid.astype(f32)).astype(bf)


def _in_proj_kernel(x_ref, w_ref, wlg_ref, wup_ref, b_ref, o_ref, cum_ref):
    x = x_ref[...].astype(jnp.bfloat16)
    o_ref[...] = jnp.dot(x, w_ref[...], preferred_element_type=jnp.float32).astype(o_ref.dtype)

    @pl.when(pl.program_id(1) == 0)
    def _():
        f32 = jnp.float32
        dot = functools.partial(jnp.dot, preferred_element_type=f32)
        lg_hi, lg_mid, _ = _split_bf16(dot(x, wlg_ref[...]))
        w_hi, w_mid, _ = _split_bf16(wup_ref[...])
        z = dot(lg_hi, w_hi) + (dot(lg_hi, w_mid) + dot(lg_mid, w_hi)) + b_ref[...]
        g = (jnp.minimum(z, 0.0) - jnp.log(1.0 + jnp.exp(-jnp.abs(z)))) * (1.0 / GLA_GATE_TAU)
        r_i = lax.broadcasted_iota(jnp.int32, (CUM_BLOCK, CUM_BLOCK), 0)
        c_i = lax.broadcasted_iota(jnp.int32, (CUM_BLOCK, CUM_BLOCK), 1)
        tri = jnp.where((r_i >= c_i) & (r_i // GLA_CHUNK == c_i // GLA_CHUNK), 1.0, 0.0).astype(jnp.bfloat16)
        for c in range(x.shape[0] // CUM_BLOCK):
            sl = slice(c * CUM_BLOCK, (c + 1) * CUM_BLOCK)
            g_hi, g_mid, g_lo = _split_bf16(g[sl, :])
            cum_ref[sl, :] = dot(tri, g_hi) + (dot(tri, g_mid) + dot(tri, g_lo))


def in_proj(x, w, w_lg, w_up, b, tm, tn):
    m, k = x.shape
    n = w.shape[1]
    assert m % tm == 0 and n % tn == 0 and tm % CUM_BLOCK == 0 and CUM_BLOCK % GLA_CHUNK == 0
    const = lambda r, c: pl.BlockSpec((r, c), lambda i, j: (0, 0))
    return pl.pallas_call(
        _in_proj_kernel,
        grid=(m // tm, n // tn),
        in_specs=[pl.BlockSpec((tm, k), lambda i, j: (i, 0)),
                  pl.BlockSpec((k, tn), lambda i, j: (0, j)),
                  const(k, LANES), const(LANES, GLA_WIDTH), const(1, GLA_WIDTH)],
        out_specs=[pl.BlockSpec((tm, tn), lambda i, j: (i, j)),
                   pl.BlockSpec((tm, GLA_WIDTH), lambda i, j: (i, 0))],
        out_shape=[jax.ShapeDtypeStruct((m, n), jnp.bfloat16), jax.ShapeDtypeStruct((m, GLA_WIDTH), jnp.float32)],
        compiler_params=_params(("parallel", "arbitrary")),
        name="in_proj",
    )(x, w, w_lg, w_up, b)


def _rope_kernel(q_ref, k_ref, v_ref, c_ref, a_ref, b_ref, qt_ref, kp_ref, vt_ref, km_ref, *, per_seq):
    half = MOBA_HEAD_DIM
    blk = pl.program_id(0) % per_seq
    lane = lax.broadcasted_iota(jnp.int32, (1, LANES), 1)
    lower = lane < half
    ind = jnp.where(lane == half + blk, 1.0, 0.0)
    zeros = jnp.zeros((LANES - half, MOBA_BLOCK), jnp.float32)
    ones = jnp.ones((LANES - half, MOBA_BLOCK), jnp.float32)

    def rot(t):
        return (t * c_ref[...] + pltpu.roll(t, LANES - ROPE_DIM // 2, 1) * a_ref[...]
                + pltpu.roll(t, ROPE_DIM // 2, 1) * b_ref[...])

    for hp in range(MOBA_WIDTH // LANES):
        sl = slice(hp * LANES, (hp + 1) * LANES)
        f32 = jnp.float32
        q_t = (rot(q_ref[:, sl].astype(f32)) * (MOBA_HEAD_DIM ** -0.5)).T
        v_t = v_ref[:, sl].astype(f32).T
        kr = rot(k_ref[:, sl].astype(f32))
        km = jnp.sum(kr, axis=0, keepdims=True) * (1.0 / MOBA_BLOCK)
        for j, (kh, kmh) in enumerate(((kr, km), (pltpu.roll(kr, half, 1), pltpu.roll(km, half, 1)))):
            h = 2 * hp + j
            rows = slice(j * half, (j + 1) * half)
            qt_ref[0, h] = jnp.concatenate([q_t[rows], zeros], axis=0).astype(qt_ref.dtype)
            vt_ref[0, h, 0] = jnp.concatenate([v_t[rows], ones], axis=0).astype(vt_ref.dtype)
            kp_ref[0, h, 0] = jnp.where(lower, kh, ind).astype(kp_ref.dtype)
            km_ref[0, 0, h:h + 1, :] = jnp.where(lower, kmh, 0.0)


def rope_prepass(proj, tab_c, tab_a, tab_b, batch, seq):
    per_seq = seq // MOBA_BLOCK
    row = lambda cb: pl.BlockSpec((MOBA_BLOCK, MOBA_WIDTH), lambda i: (i, cb))
    tab = pl.BlockSpec((MOBA_BLOCK, LANES), lambda i: (i % per_seq, 0))
    bf = jnp.bfloat16
    return pl.pallas_call(
        functools.partial(_rope_kernel, per_seq=per_seq),
        grid=(batch * per_seq,),
        in_specs=[row(COL_QA // MOBA_WIDTH), row(COL_KA // MOBA_WIDTH), row(COL_VA // MOBA_WIDTH), tab, tab, tab],
        out_specs=[pl.BlockSpec((1, MOBA_HEADS, LANES, MOBA_BLOCK), lambda i: (i // per_seq, 0, 0, i % per_seq)),
                   pl.BlockSpec((1, MOBA_HEADS, 1, MOBA_BLOCK, LANES), lambda i: (i // per_seq, 0, i % per_seq, 0, 0)),
                   pl.BlockSpec((1, MOBA_HEADS, 1, LANES, MOBA_BLOCK), lambda i: (i // per_seq, 0, i % per_seq, 0, 0)),
                   pl.BlockSpec((1, 1, MOBA_HEADS, LANES), lambda i: (i // per_seq, i % per_seq, 0, 0))],
        out_shape=[jax.ShapeDtypeStruct((batch, MOBA_HEADS, LANES, seq), bf),
                   jax.ShapeDtypeStruct((batch, MOBA_HEADS, per_seq, MOBA_BLOCK, LANES), bf),
                   jax.ShapeDtypeStruct((batch, MOBA_HEADS, per_seq, LANES, MOBA_BLOCK), bf),
                   jax.ShapeDtypeStruct((batch, per_seq, MOBA_HEADS, LANES), jnp.float32)],
        compiler_params=_params(("parallel",)),
        name="rope_kmean",
    )(proj, proj, proj, tab_c, tab_a, tab_b)


HEADS_PER_STEP = 8


def _moba_kernel(qt_ref, kp_ref, vt_ref, km_ref, o_ref, qb_ref, p_ref, acc_ref, *, nb_pad):
    own = pl.program_id(2)
    half = MOBA_HEAD_DIM
    tq = MOBA_BLOCK
    bf = jnp.bfloat16
    blk = lax.broadcasted_iota(jnp.int32, (nb_pad, tq), 0)
    causal = lax.broadcasted_iota(jnp.int32, (tq, tq), 0) <= lax.broadcasted_iota(jnp.int32, (tq, tq), 1)

    def weights(h, j, m):
        kp = kp_ref[0, h, pl.ds(2 * j, 2)].reshape(2 * MOBA_BLOCK, LANES)
        s = jnp.dot(kp, qb_ref[h], preferred_element_type=jnp.float32)
        m_new = jnp.maximum(m, jnp.max(s, axis=0, keepdims=True))
        return m_new, jnp.exp(s - m_new).astype(bf), jnp.exp(m - m_new)

    def absorb(h, j, alpha):
        return (alpha * acc_ref[h]
                + jnp.dot(vt_ref[0, h, 2 * j], p_ref[h, :MOBA_BLOCK], preferred_element_type=jnp.float32)
                + jnp.dot(vt_ref[0, h, 2 * j + 1], p_ref[h, MOBA_BLOCK:], preferred_element_type=jnp.float32))

    carry = []
    for h in range(HEADS_PER_STEP):
        qt = qt_ref[0, h]
        gate = jnp.dot(km_ref[0, h], qt.astype(jnp.float32), precision=lax.Precision.HIGHEST,
                       preferred_element_type=jnp.float32)
        g = jnp.where(blk < own, gate, -jnp.inf)
        sel = jnp.zeros(g.shape, jnp.float32)
        for _ in range(MOBA_TOP_BLOCKS):
            m = jnp.max(g, axis=0, keepdims=True)
            idx = jnp.min(jnp.where(g == m, blk, nb_pad), axis=0, keepdims=True)
            hit = blk == idx
            sel = jnp.where(hit & (m > -jnp.inf), 1.0, sel)
            g = jnp.where(hit, -jnp.inf, g)
        bias = jnp.where(sel > 0.0, 0.0, MASK_VALUE).astype(bf)
        qb_ref[h] = jnp.concatenate([qt[:half], bias, jnp.zeros((LANES - half - nb_pad, tq), bf)], axis=0)

        s = jnp.dot(kp_ref[0, h, own], qt, preferred_element_type=jnp.float32)
        s = jnp.where(causal, s, MASK_VALUE)
        m = jnp.max(s, axis=0, keepdims=True)
        p = jnp.exp(s - m)
        acc_ref[h] = jnp.dot(vt_ref[0, h, own], p.astype(bf), preferred_element_type=jnp.float32)
        m, p_ref[h], alpha = weights(h, 0, m)
        carry += [m, alpha]

    def step(j, carry):
        new = []
        for h in range(HEADS_PER_STEP):
            m, alpha = carry[2 * h:2 * h + 2]
            acc_ref[h] = absorb(h, j - 1, alpha)
            m, p_ref[h], alpha = weights(h, j, m)
            new += [m, alpha]
        return tuple(new)

    n_pairs = jnp.maximum((own + 1) // 2, 1)
    carry = lax.fori_loop(1, n_pairs, step, tuple(carry))
    outs = []
    for h in range(HEADS_PER_STEP):
        acc = absorb(h, n_pairs - 1, carry[2 * h + 1])
        outs.append(acc[:half] / acc[half:half + 1])
    o_ref[...] = jnp.concatenate(outs, axis=0).T.astype(o_ref.dtype)


def moba_attention(qt, kp, vt, km, batch, seq):
    nq = seq // MOBA_BLOCK
    nb_pad = km.shape[2]
    hs = HEADS_PER_STEP
    return pl.pallas_call(
        functools.partial(_moba_kernel, nb_pad=nb_pad),
        grid=(batch, MOBA_HEADS // hs, nq),
        in_specs=[pl.BlockSpec((1, hs, LANES, MOBA_BLOCK), lambda b, hp, i: (b, hp, 0, i)),
                  pl.BlockSpec((1, hs, nq, MOBA_BLOCK, LANES), lambda b, hp, i: (b, hp, 0, 0, 0)),
                  pl.BlockSpec((1, hs, nq, LANES, MOBA_BLOCK), lambda b, hp, i: (b, hp, 0, 0, 0)),
                  pl.BlockSpec((1, hs, nb_pad, LANES), lambda b, hp, i: (b, hp, 0, 0))],
        out_specs=pl.BlockSpec((MOBA_BLOCK, hs * MOBA_HEAD_DIM), lambda b, hp, i: (b * nq + i, hp)),
        out_shape=jax.ShapeDtypeStruct((batch * seq, MOBA_WIDTH), jnp.bfloat16),
        scratch_shapes=[pltpu.VMEM((hs, LANES, MOBA_BLOCK), jnp.bfloat16),
                        pltpu.VMEM((hs, 2 * MOBA_BLOCK, MOBA_BLOCK), jnp.bfloat16),
                        pltpu.VMEM((hs, LANES, MOBA_BLOCK), jnp.float32)],
        compiler_params=_params(("parallel", "parallel", "arbitrary")),
        name="moba_attention",
    )(qt, kp, vt, km)


def _gla_kernel(q_ref, k_ref, v_ref, r_ref, cum_ref, gain_ref, o_ref, state_ref, *, rows):
    @pl.when(pl.program_id(1) == 0)
    def _():
        state_ref[...] = jnp.zeros_like(state_ref)

    c = GLA_CHUNK
    r_i = lax.broadcasted_iota(jnp.int32, (c, c), 0)
    c_i = lax.broadcasted_iota(jnp.int32, (c, c), 1)
    lower = r_i >= c_i
    nt = (((1,), (1,)), ((), ()))
    tn = (((0,), (0,)), ((), ()))
    bf = jnp.bfloat16
    f32 = jnp.float32
    for j in range(rows // c):
        sl = slice(j * c, (j + 1) * c)
        for h in range(GLA_HEADS):
            hl = slice(h * LANES, (h + 1) * LANES)
            cum = cum_ref[sl, hl]
            cum_last = cum[c - 1:c, :]
            q = q_ref[sl, hl].astype(f32) * (GLA_KEY_DIM ** -0.5)
            k = k_ref[sl, hl].astype(f32)
            v = v_ref[sl, hl]
            q_dec = (q * jnp.exp(cum)).astype(bf)
            k_inv = (k * jnp.exp(-cum)).astype(bf)
            k_end = (k * jnp.exp(cum_last - cum)).astype(bf)
            att = lax.dot_general(q_dec, k_inv, nt, preferred_element_type=f32)
            att = jnp.where(lower, att, 0.0)
            intra = jnp.dot(att.astype(bf), v, preferred_element_type=f32)
            state = state_ref[h]
            inter = lax.dot_general(q_dec, state.astype(bf), nt, preferred_element_type=f32)
            d_state = lax.dot_general(v, k_end, tn, preferred_element_type=f32)
            state_ref[h] = state * jnp.exp(cum_last) + d_state
            o = intra + inter
            mu = jnp.mean(o, axis=-1, keepdims=True)
            var = jnp.mean(jnp.square(o - mu), axis=-1, keepdims=True)
            on = (o - mu) * lax.rsqrt(var + LN_EPS) * gain_ref[:, hl]
            rg = r_ref[sl, hl].astype(f32)
            o_ref[sl, hl] = (on * (rg * jax.nn.sigmoid(rg))).astype(o_ref.dtype)


def gla(proj, cum, gain, batch, seq, rows):
    t = proj.shape[0]
    assert seq % rows == 0 and rows % GLA_CHUNK == 0
    ns = seq // rows
    col = lambda c0: pl.BlockSpec((rows, GLA_WIDTH), lambda b, i: (b * ns + i, c0 // GLA_WIDTH))
    return pl.pallas_call(
        functools.partial(_gla_kernel, rows=rows),
        grid=(batch, ns),
        in_specs=[col(COL_QG), col(COL_KG), col(COL_VG), col(COL_RG), col(0),
                  pl.BlockSpec((1, GLA_WIDTH), lambda b, i: (0, 0))],
        out_specs=col(0),
        out_shape=jax.ShapeDtypeStruct((t, GLA_WIDTH), jnp.bfloat16),
        scratch_shapes=[pltpu.VMEM((GLA_HEADS, LANES, LANES), jnp.float32)],
        compiler_params=_params(("parallel", "arbitrary")),
        name="gla",
    )(proj, proj, proj, proj, cum, gain)


def _layer_norm(z, g, b):
    mu = jnp.mean(z, axis=-1, keepdims=True)
    var = jnp.mean(jnp.square(z - mu), axis=-1, keepdims=True)
    return (z - mu) * lax.rsqrt(var + LN_EPS) * g + b


def _merge_kernel(ya_ref, yb_ref, ga_ref, gb_ref, x_ref, wa_ref, wb_ref, wo_ref, g_ref, b_ref, xo_ref, xb_ref):
    bf = jnp.bfloat16
    f32 = jnp.float32
    pa = jnp.dot(ya_ref[...], wa_ref[...], preferred_element_type=f32)
    pb = jnp.dot(yb_ref[...], wb_ref[...], preferred_element_type=f32)
    merged = jax.nn.sigmoid(ga_ref[...].astype(f32)) * pa + jax.nn.sigmoid(gb_ref[...].astype(f32)) * pb
    mix = jnp.dot(merged.astype(bf), wo_ref[...], preferred_element_type=jnp.float32)
    y = _layer_norm(DEEPNORM_ALPHA * x_ref[...] + mix, g_ref[...], b_ref[...])
    xo_ref[...] = y
    xb_ref[...] = y.astype(bf)


def merge_out(ya, yb, proj, x, wa, wb, wo, g, b, tm):
    t = x.shape[0]
    row = lambda w, cb=0: pl.BlockSpec((tm, w), lambda i: (i, cb))
    const = lambda r, c: pl.BlockSpec((r, c), lambda i: (0, 0))
    return pl.pallas_call(
        _merge_kernel,
        grid=(t // tm,),
        in_specs=[row(MOBA_WIDTH), row(GLA_WIDTH), row(D_MODEL, COL_GATE_A // D_MODEL), row(D_MODEL, COL_GATE_B // D_MODEL),
                  row(D_MODEL), const(MOBA_WIDTH, D_MODEL), const(GLA_WIDTH, D_MODEL), const(D_MODEL, D_MODEL),
                  const(1, D_MODEL), const(1, D_MODEL)],
        out_specs=[row(D_MODEL), row(D_MODEL)],
        out_shape=[jax.ShapeDtypeStruct((t, D_MODEL), jnp.float32), jax.ShapeDtypeStruct((t, D_MODEL), jnp.bfloat16)],
        compiler_params=_params(("parallel",)),
        name="merge_out_ln",
    )(ya, yb, proj, proj, x, wa, wb, wo, g, b)


def _ffn_kernel(xb_ref, x_ref, wg_ref, wu_ref, wd_ref, g_ref, b_ref, xo_ref, xbo_ref, acc_ref):
    f = pl.program_id(1)

    @pl.when(f == 0)
    def _():
        acc_ref[...] = jnp.zeros_like(acc_ref)

    h = xb_ref[...]
    gate = jnp.dot(h, wg_ref[...], preferred_element_type=jnp.float32)
    up = jnp.dot(h, wu_ref[...], preferred_element_type=jnp.float32)
    hid = gate * jax.nn.sigmoid(gate) * up
    acc_ref[...] += jnp.dot(hid.astype(jnp.bfloat16), wd_ref[...], preferred_element_type=jnp.float32)

    @pl.when(f == pl.num_programs(1) - 1)
    def _():
        y = _layer_norm(DEEPNORM_ALPHA * x_ref[...] + acc_ref[...], g_ref[...], b_ref[...])
        xo_ref[...] = y
        xbo_ref[...] = y.astype(jnp.bfloat16)


def ffn(xb, x, wg, wu, wd, g, b, tm, tf):
    t = x.shape[0]
    fdim = wg.shape[1]
    row = lambda w: pl.BlockSpec((tm, w), lambda i, f: (i, 0))
    const = pl.BlockSpec((1, D_MODEL), lambda i, f: (0, 0))
    return pl.pallas_call(
        _ffn_kernel,
        grid=(t // tm, fdim // tf),
        in_specs=[row(D_MODEL), row(D_MODEL),
                  pl.BlockSpec((D_MODEL, tf), lambda i, f: (0, f)),
                  pl.BlockSpec((D_MODEL, tf), lambda i, f: (0, f)),
                  pl.BlockSpec((tf, D_MODEL), lambda i, f: (f, 0)),
                  const, const],
        out_specs=[row(D_MODEL), row(D_MODEL)],
        out_shape=[jax.ShapeDtypeStruct((t, D_MODEL), jnp.float32), jax.ShapeDtypeStruct((t, D_MODEL), jnp.bfloat16)],
        scratch_shapes=[pltpu.VMEM((tm, D_MODEL), jnp.float32)],
        compiler_params=_params(("parallel", "arbitrary")),
        name="swiglu_ln",
    )(xb, x, wg, wu, wd, g, b)


ROUTE_E1, ROUTE_E2, ROUTE_W1, ROUTE_W2 = 0, 1, 2, 3


def _router_kernel(x_ref, w_ref, o_ref):
    logits = jnp.dot(x_ref[...], w_ref[...], precision=lax.Precision.HIGHEST, preferred_element_type=jnp.float32)
    lane = lax.broadcasted_iota(jnp.int32, (1, LANES), 1)
    g = jnp.where(lane < N_EXPERTS, logits, -jnp.inf)
    m1 = jnp.max(g, axis=-1, keepdims=True)
    i1 = jnp.min(jnp.where(g == m1, lane, LANES), axis=-1, keepdims=True)
    g2 = jnp.where(lane == i1, -jnp.inf, g)
    m2 = jnp.max(g2, axis=-1, keepdims=True)
    i2 = jnp.min(jnp.where(g2 == m2, lane, LANES), axis=-1, keepdims=True)
    e2 = jnp.exp(m2 - m1)
    w1 = 1.0 / (1.0 + e2)
    w2 = e2 / (1.0 + e2)
    out = jnp.where(lane == ROUTE_E1, i1.astype(jnp.float32), 0.0)
    out = jnp.where(lane == ROUTE_E2, i2.astype(jnp.float32), out)
    out = jnp.where(lane == ROUTE_W1, w1, out)
    o_ref[...] = jnp.where(lane == ROUTE_W2, w2, out)


def router(x, w, tm):
    t = x.shape[0]
    return pl.pallas_call(
        _router_kernel,
        grid=(t // tm,),
        in_specs=[pl.BlockSpec((tm, D_MODEL), lambda i: (i, 0)), pl.BlockSpec((D_MODEL, LANES), lambda i: (0, 0))],
        out_specs=pl.BlockSpec((tm, LANES), lambda i: (i, 0)),
        out_shape=jax.ShapeDtypeStruct((t, LANES), jnp.float32),
        compiler_params=_params(("parallel",)),
        name="router",
    )(x, w)


def _routing_tables(route, tm):
    t = route.shape[0]
    experts = route[:, ROUTE_E1:ROUTE_E2 + 1].astype(jnp.int32).reshape(-1)
    onehot = (experts[:, None] == jnp.arange(N_EXPERTS, dtype=jnp.int32)[None, :]).astype(jnp.int32)
    csum = jnp.cumsum(onehot, axis=0)
    rank = jnp.sum((csum - onehot) * onehot, axis=1)
    size = -(-csum[-1] // tm) * tm
    end = jnp.cumsum(size)
    pos = jnp.sum(onehot * (end - size)[None, :], axis=1) + rank
    n_tiles = 2 * t // tm + N_EXPERTS
    tok = jnp.zeros(((n_tiles + 1) * tm,), jnp.int32).at[pos].set(
        jnp.arange(2 * t, dtype=jnp.int32) // 2, unique_indices=True, mode="promise_in_bounds")
    tile_start = jnp.arange(n_tiles, dtype=jnp.int32) * tm
    tile_expert = jnp.minimum(jnp.sum((tile_start[:, None] >= end[None, :]).astype(jnp.int32), axis=1), N_EXPERTS - 1)
    n_used = end[-1] // tm
    return tok.reshape(n_tiles + 1, tm), pos.reshape(t, 2), tile_expert, n_used.reshape(1)


SUBLANES = 8


def _issue_row_copies(slot, n_rows, idx_ref, src_hbm, dst_ref, sem):
    for s in range(2):
        @pl.when(slot == s)
        def _():
            def body(j, c):
                base = pl.multiple_of(j * SUBLANES, SUBLANES)
                for u in range(SUBLANES):
                    pltpu.make_async_copy(src_hbm.at[pl.ds(idx_ref[s, base + u], 1), :],
                                          dst_ref.at[s, pl.ds(base + u, 1), :], sem.at[s]).start()
                return c
            lax.fori_loop(0, n_rows // SUBLANES, body, 0)


def _moe_kernel(te_ref, nu_ref, tok_hbm, x_hbm, wg_ref, wu_ref, wd_ref, y_ref,
                idx_ref, xg_ref, xb_ref, acc_ref, idx_sem, row_sem, *, tm):
    i = pl.program_id(0)
    f = pl.program_id(1)
    nt = pl.num_programs(0)
    nf = pl.num_programs(1)
    slot = i % 2
    nxt = 1 - slot
    chunk = tm // (nf + 1)

    def idx_copy(tile, s):
        return pltpu.make_async_copy(tok_hbm.at[tile], idx_ref.at[s], idx_sem.at[s])

    def row_copy(s, r):
        return pltpu.make_async_copy(x_hbm.at[pl.ds(idx_ref[s, r], 1), :], xg_ref.at[s, pl.ds(r, 1), :], row_sem.at[s])

    def rows_done(s):
        return pltpu.make_async_copy(x_hbm.at[pl.ds(0, tm), :], xg_ref.at[s], row_sem.at[s])

    def issue_chunk():
        first = (f + 1) * chunk
        for u in range(chunk):
            row_copy(nxt, first + u).start()

    @pl.when(f == 0)
    def _():
        @pl.when(i == 0)
        def _():
            idx_copy(0, 0).start()
            idx_copy(0, 0).wait()
            _issue_row_copies(0, tm, idx_ref, x_hbm, xg_ref, row_sem)
            idx_copy(1, 1).start()

        idx_copy(i + 1, nxt).wait()
        _issue_row_copies(nxt, chunk, idx_ref, x_hbm, xg_ref, row_sem)
        rows_done(slot).wait()
        xb_ref[...] = xg_ref[slot].astype(jnp.bfloat16)
        acc_ref[...] = jnp.zeros_like(acc_ref)

    @pl.when((f == 1) & (i + 1 < nt))
    def _():
        idx_copy(i + 2, slot).start()

    valid = i < nu_ref[0]

    @pl.when(valid)
    def _():
        issue_chunk()
        h = xb_ref[...]
        bf = jnp.bfloat16
        gate = jnp.dot(h, wg_ref[0].astype(bf), preferred_element_type=jnp.float32)
        up = jnp.dot(h, wu_ref[0].astype(bf), preferred_element_type=jnp.float32)
        hid = gate * jax.nn.sigmoid(gate) * up
        acc_ref[...] += jnp.dot(hid.astype(bf), wd_ref[0].astype(bf), preferred_element_type=jnp.float32)

    @pl.when(jnp.logical_not(valid))
    def _():
        issue_chunk()

    @pl.when(f == nf - 1)
    def _():
        y_ref[...] = acc_ref[...]

        @pl.when(i == nt - 1)
        def _():
            rows_done(nxt).wait()


def moe_experts(tile_expert, n_used, tok, x, wg, wu, wd, tm, tf):
    n_tiles = tok.shape[0] - 1
    fdim = wg.shape[2]
    assert fdim // tf >= 2 and n_tiles >= 2 and tm % (SUBLANES * (fdim // tf + 1)) == 0
    grid_spec = pltpu.PrefetchScalarGridSpec(
        num_scalar_prefetch=2,
        grid=(n_tiles, fdim // tf),
        in_specs=[pl.BlockSpec(memory_space=pl.ANY), pl.BlockSpec(memory_space=pl.ANY),
                  pl.BlockSpec((1, D_MODEL, tf), lambda i, f, te, nu: (te[i], 0, f)),
                  pl.BlockSpec((1, D_MODEL, tf), lambda i, f, te, nu: (te[i], 0, f)),
                  pl.BlockSpec((1, tf, D_MODEL), lambda i, f, te, nu: (te[i], f, 0))],
        out_specs=pl.BlockSpec((tm, D_MODEL), lambda i, f, te, nu: (i, 0)),
        scratch_shapes=[pltpu.SMEM((2, tm), jnp.int32), pltpu.VMEM((2, tm, D_MODEL), jnp.float32),
                        pltpu.VMEM((tm, D_MODEL), jnp.bfloat16), pltpu.VMEM((tm, D_MODEL), jnp.float32),
                        pltpu.SemaphoreType.DMA((2,)), pltpu.SemaphoreType.DMA((2,))])
    return pl.pallas_call(
        functools.partial(_moe_kernel, tm=tm),
        grid_spec=grid_spec,
        out_shape=jax.ShapeDtypeStruct((n_tiles * tm, D_MODEL), jnp.float32),
        compiler_params=_params(("arbitrary", "arbitrary"), disable_bounds_checks=True),
        name="moe_experts",
    )(tile_expert, n_used, tok, x, wg, wu, wd)


def _combine_kernel(pos_hbm, y_hbm, route_ref, x_ref, g_ref, b_ref, xo_ref, idx_ref, rows_ref, idx_sem, row_sem, *, tc):
    i = pl.program_id(0)
    nt = pl.num_programs(0)
    slot = i % 2

    def idx_copy(tile, s):
        return pltpu.make_async_copy(pos_hbm.at[tile], idx_ref.at[s], idx_sem.at[s])

    def gather_rows(s):
        _issue_row_copies(s, 2 * tc, idx_ref, y_hbm, rows_ref, row_sem)

    @pl.when(i == 0)
    def _():
        idx_copy(0, 0).start()
        idx_copy(0, 0).wait()
        gather_rows(0)
        idx_copy(1, 1).start()

    @pl.when(i + 1 < nt)
    def _():
        idx_copy(i + 1, 1 - slot).wait()
        gather_rows(1 - slot)

    pltpu.make_async_copy(y_hbm.at[pl.ds(0, 2 * tc), :], rows_ref.at[slot], row_sem.at[slot]).wait()

    @pl.when(i + 2 < nt)
    def _():
        idx_copy(i + 2, slot).start()

    route = route_ref[...]
    w1 = route[:, ROUTE_W1:ROUTE_W1 + 1]
    w2 = route[:, ROUTE_W2:ROUTE_W2 + 1]
    f = w1 * rows_ref[slot, :tc, :] + w2 * rows_ref[slot, tc:, :]
    xo_ref[...] = _layer_norm(DEEPNORM_ALPHA * x_ref[...] + f, g_ref[...], b_ref[...])


def moe_combine(pos, y, route, x, g, b, tc):
    t = x.shape[0]
    n_tiles = t // tc
    assert n_tiles >= 2
    pos_tiles = pos.reshape(n_tiles, tc, 2).transpose(0, 2, 1).reshape(n_tiles, 2 * tc)
    const = pl.BlockSpec((1, D_MODEL), lambda i: (0, 0))
    return pl.pallas_call(
        functools.partial(_combine_kernel, tc=tc),
        grid=(n_tiles,),
        in_specs=[pl.BlockSpec(memory_space=pl.ANY), pl.BlockSpec(memory_space=pl.ANY),
                  pl.BlockSpec((tc, LANES), lambda i: (i, 0)), pl.BlockSpec((tc, D_MODEL), lambda i: (i, 0)),
                  const, const],
        out_specs=pl.BlockSpec((tc, D_MODEL), lambda i: (i, 0)),
        out_shape=jax.ShapeDtypeStruct((t, D_MODEL), jnp.float32),
        scratch_shapes=[pltpu.SMEM((2, 2 * tc), jnp.int32), pltpu.VMEM((2, 2 * tc, D_MODEL), jnp.float32),
                        pltpu.SemaphoreType.DMA((2,)), pltpu.SemaphoreType.DMA((2,))],
        compiler_params=_params(("arbitrary",), disable_bounds_checks=True),
        name="moe_combine_ln",
    )(pos_tiles, y, route, x, g, b)


def _rope_tables(seq):
    half = ROPE_DIM // 2
    inv_freq = jnp.power(ROPE_THETA, -jnp.arange(0, ROPE_DIM, 2, dtype=jnp.float32) / ROPE_DIM)
    ang = jnp.arange(seq, dtype=jnp.float32)[:, None] * inv_freq[None, :]
    cos, sin = jnp.cos(ang), jnp.sin(ang)
    ones = jnp.ones((seq, MOBA_HEAD_DIM - ROPE_DIM), jnp.float32)
    zeros = jnp.zeros((seq, MOBA_HEAD_DIM - ROPE_DIM), jnp.float32)
    z8 = jnp.zeros((seq, half), jnp.float32)
    tab_c = jnp.concatenate([cos, cos, ones], axis=1)
    tab_a = jnp.concatenate([-sin, z8, zeros], axis=1)
    tab_b = jnp.concatenate([z8, sin, zeros], axis=1)
    dup = lambda t: jnp.concatenate([t, t], axis=1)
    return dup(tab_c), dup(tab_a), dup(tab_b)


def _split_w_in(w):
    offs = np.cumsum([0, 512, 512, 512, 512, 512, 512, 512, GLA_GATE_RANK, 1024, 1024])
    part = lambda j: w[:, int(offs[j]):int(offs[j + 1])]
    qa, ka, va, qg, kg, vg, rg, lg, gate_a, gate_b = [part(j) for j in range(10)]
    main = jnp.concatenate([gate_a, gate_b, qa, ka, va, qg, kg, vg, rg], axis=1).astype(jnp.bfloat16)
    lg = jnp.pad(lg, ((0, 0), (0, LANES - GLA_GATE_RANK))).astype(jnp.bfloat16)
    return main, lg


def kernel(x, w_in, w_gla_gate_up, b_gla_gate, gla_norm_g, w_branch_a, w_branch_b, w_out, ln_mix_g, ln_mix_b,
           ffn_w_gate, ffn_w_up, ffn_w_down, moe_w_router, moe_w_gate, moe_w_up, moe_w_down, ln_ffn_g, ln_ffn_b):
    batch, seq, d = x.shape
    t = batch * seq
    bf16_rows = 16
    nb_pad = -(-(seq // MOBA_BLOCK) // bf16_rows) * bf16_rows
    assert d == D_MODEL and seq % MOBA_BLOCK == 0 and nb_pad <= LANES - MOBA_HEAD_DIM
    bf = jnp.bfloat16
    tab_c, tab_a, tab_b = _rope_tables(seq)
    xf = x.reshape(t, d)
    xb = xf
    row_tile = math.gcd(t, 1024)
    for layer in range(DEPTH):
        w_main, w_lg = _split_w_in(w_in[layer])
        w_up = jnp.pad(w_gla_gate_up[layer], ((0, LANES - GLA_GATE_RANK), (0, 0)))
        proj, cum = in_proj(xb, w_main, w_lg, w_up, b_gla_gate[layer].reshape(1, -1), math.gcd(t, 2048), 512)
        qt, kp, vt, kmean = rope_prepass(proj, tab_c, tab_a, tab_b, batch, seq)
        km = jnp.pad(kmean.transpose(0, 2, 1, 3), ((0, 0), (0, 0), (0, nb_pad - seq // MOBA_BLOCK), (0, 0)))
        y_a = moba_attention(qt, kp, vt, km, batch, seq)
        y_b = gla(proj, cum, gla_norm_g[layer].reshape(1, -1), batch, seq, 512)
        xf, xb = merge_out(y_a, y_b, proj, xf, w_branch_a[layer].astype(bf), w_branch_b[layer].astype(bf),
                           w_out[layer].astype(bf), ln_mix_g[layer].reshape(1, -1), ln_mix_b[layer].reshape(1, -1),
                           math.gcd(t, 512))
        j = layer // 2
        ln_g, ln_b = ln_ffn_g[layer].reshape(1, -1), ln_ffn_b[layer].reshape(1, -1)
        if layer % 2 == 0:
            xf, xb = ffn(xb, xf, ffn_w_gate[j].astype(bf), ffn_w_up[j].astype(bf), ffn_w_down[j].astype(bf),
                         ln_g, ln_b, row_tile, 512)
        else:
            w_r = jnp.pad(moe_w_router[j], ((0, 0), (0, LANES - N_EXPERTS)))
            route = router(xf, w_r, math.gcd(t, 512))
            tok, pos, tile_expert, n_used = _routing_tables(route, row_tile)
            y = moe_experts(tile_expert, n_used, tok, xf, moe_w_gate[j], moe_w_up[j], moe_w_down[j], row_tile, 512)
            xf = moe_combine(pos, y, route, xf, ln_g, ln_b, math.gcd(t, 512))
    return xf.reshape(batch, seq, d)
```

```python
import functools
import math

import jax
import jax.numpy as jnp
import numpy as np
from jax import lax
from jax.experimental import pallas as pl
from jax.experimental.pallas import tpu as pltpu

D_MODEL = 1024
DEPTH = 2
MOBA_HEADS = 8
MOBA_HEAD_DIM = 64
MOBA_WIDTH = 512
MOBA_BLOCK = 256
MOBA_TOP_BLOCKS = 3
ROPE_THETA = 500000.0
ROPE_DIM = 16
GLA_HEADS = 4
GLA_KEY_DIM = 128
GLA_WIDTH = 512
GLA_GATE_RANK = 16
GLA_GATE_TAU = 16.0
GLA_CHUNK = 64
FFN_DIM = 3584
N_EXPERTS = 8
DEEPNORM_ALPHA = (2 * DEPTH) ** 0.25
LN_EPS = 1e-5
MASK_VALUE = -1e30

LANES = 128
VMEM_LIMIT = 56 * 1024 * 1024

COL_GATE_A, COL_GATE_B = 0, 1024
COL_QA, COL_KA, COL_VA = 2048, 2560, 3072
COL_QG, COL_KG, COL_VG, COL_RG = 3584, 4096, 4608, 5120
MAIN_COLS = 5632


def _params(sem, **kw):
    return pltpu.CompilerParams(dimension_semantics=sem, vmem_limit_bytes=VMEM_LIMIT, **kw)


CUM_BLOCK = 256


def _split_bf16(a):
    f32, bf = jnp.float32, jnp.bfloat16
    hi = a.astype(bf)
    r = a - hi.astype(f32)
    mid = r.astype(bf)
    return hi, mid, (r - mid.astype(f32)).astype(bf)


def _in_proj_kernel(x_ref, w_ref, wlg_ref, wup_ref, b_ref, o_ref, cum_ref):
    x = x_ref[...].astype(jnp.bfloat16)
    o_ref[...] = jnp.dot(x, w_ref[...], preferred_element_type=jnp.float32).astype(o_ref.dtype)

    @pl.when(pl.program_id(1) == 0)
    def _():
        f32 = jnp.float32
        dot = functools.partial(jnp.dot, preferred_element_type=f32)
        lg_hi, lg_mid, _ = _split_bf16(dot(x, wlg_ref[...]))
        w_hi, w_mid, _ = _split_bf16(wup_ref[...])
        z = dot(lg_hi, w_hi) + (dot(lg_hi, w_mid) + dot(lg_mid, w_hi)) + b_ref[...]
        g = (jnp.minimum(z, 0.0) - jnp.log(1.0 + jnp.exp(-jnp.abs(z)))) * (1.0 / GLA_GATE_TAU)
        r_i = lax.broadcasted_iota(jnp.int32, (CUM_BLOCK, CUM_BLOCK), 0)
        c_i = lax.broadcasted_iota(jnp.int32, (CUM_BLOCK, CUM_BLOCK), 1)
        tri = jnp.where((r_i >= c_i) & (r_i // GLA_CHUNK == c_i // GLA_CHUNK), 1.0, 0.0).astype(jnp.bfloat16)
        for c in range(x.shape[0] // CUM_BLOCK):
            sl = slice(c * CUM_BLOCK, (c + 1) * CUM_BLOCK)
            g_hi, g_mid, g_lo = _split_bf16(g[sl, :])
            cum_ref[sl, :] = dot(tri, g_hi) + (dot(tri, g_mid) + dot(tri, g_lo))


def in_proj(x, w, w_lg, w_up, b, tm, tn):
    m, k = x.shape
    n = w.shape[1]
    assert m % tm == 0 and n % tn == 0 and tm % CUM_BLOCK == 0 and CUM_BLOCK % GLA_CHUNK == 0
    const = lambda r, c: pl.BlockSpec((r, c), lambda i, j: (0, 0))
    return pl.pallas_call(
        _in_proj_kernel,
        grid=(m // tm, n // tn),
        in_specs=[pl.BlockSpec((tm, k), lambda i, j: (i, 0)),
                  pl.BlockSpec((k, tn), lambda i, j: (0, j)),
                  const(k, LANES), const(LANES, GLA_WIDTH), const(1, GLA_WIDTH)],
        out_specs=[pl.BlockSpec((tm, tn), lambda i, j: (i, j)),
                   pl.BlockSpec((tm, GLA_WIDTH), lambda i, j: (i, 0))],
        out_shape=[jax.ShapeDtypeStruct((m, n), jnp.bfloat16), jax.ShapeDtypeStruct((m, GLA_WIDTH), jnp.float32)],
        compiler_params=_params(("parallel", "arbitrary")),
        name="in_proj",
    )(x, w, w_lg, w_up, b)


def _rope_kernel(q_ref, k_ref, v_ref, c_ref, a_ref, b_ref, qt_ref, kp_ref, vt_ref, km_ref, *, per_seq):
    half = MOBA_HEAD_DIM
    blk = pl.program_id(0) % per_seq
    lane = lax.broadcasted_iota(jnp.int32, (1, LANES), 1)
    lower = lane < half
    ind = jnp.where(lane == half + blk, 1.0, 0.0)
    zeros = jnp.zeros((LANES - half, MOBA_BLOCK), jnp.float32)
    ones = jnp.ones((LANES - half, MOBA_BLOCK), jnp.float32)

    def rot(t):
        return (t * c_ref[...] + pltpu.roll(t, LANES - ROPE_DIM // 2, 1) * a_ref[...]
                + pltpu.roll(t, ROPE_DIM // 2, 1) * b_ref[...])

    for hp in range(MOBA_WIDTH // LANES):
        sl = slice(hp * LANES, (hp + 1) * LANES)
        f32 = jnp.float32
        q_t = (rot(q_ref[:, sl].astype(f32)) * (MOBA_HEAD_DIM ** -0.5)).T
        v_t = v_ref[:, sl].astype(f32).T
        kr = rot(k_ref[:, sl].astype(f32))
        km = jnp.sum(kr, axis=0, keepdims=True) * (1.0 / MOBA_BLOCK)
        for j, (kh, kmh) in enumerate(((kr, km), (pltpu.roll(kr, half, 1), pltpu.roll(km, half, 1)))):
            h = 2 * hp + j
            rows = slice(j * half, (j + 1) * half)
            qt_ref[0, h] = jnp.concatenate([q_t[rows], zeros], axis=0).astype(qt_ref.dtype)
            vt_ref[0, h, 0] = jnp.concatenate([v_t[rows], ones], axis=0).astype(vt_ref.dtype)
            kp_ref[0, h, 0] = jnp.where(lower, kh, ind).astype(kp_ref.dtype)
            km_ref[0, 0, h:h + 1, :] = jnp.where(lower, kmh, 0.0)


def rope_prepass(proj, tab_c, tab_a, tab_b, batch, seq):
    per_seq = seq // MOBA_BLOCK
    row = lambda cb: pl.BlockSpec((MOBA_BLOCK, MOBA_WIDTH), lambda i: (i, cb))
    tab = pl.BlockSpec((MOBA_BLOCK, LANES), lambda i: (i % per_seq, 0))
    bf = jnp.bfloat16
    return pl.pallas_call(
        functools.partial(_rope_kernel, per_seq=per_seq),
        grid=(batch * per_seq,),
        in_specs=[row(COL_QA // MOBA_WIDTH), row(COL_KA // MOBA_WIDTH), row(COL_VA // MOBA_WIDTH), tab, tab, tab],
        out_specs=[pl.BlockSpec((1, MOBA_HEADS, LANES, MOBA_BLOCK), lambda i: (i // per_seq, 0, 0, i % per_seq)),
                   pl.BlockSpec((1, MOBA_HEADS, 1, MOBA_BLOCK, LANES), lambda i: (i // per_seq, 0, i % per_seq, 0, 0)),
                   pl.BlockSpec((1, MOBA_HEADS, 1, LANES, MOBA_BLOCK), lambda i: (i // per_seq, 0, i % per_seq, 0, 0)),
                   pl.BlockSpec((1, 1, MOBA_HEADS, LANES), lambda i: (i // per_seq, i % per_seq, 0, 0))],
        out_shape=[jax.ShapeDtypeStruct((batch, MOBA_HEADS, LANES, seq), bf),
                   jax.ShapeDtypeStruct((batch, MOBA_HEADS, per_seq, MOBA_BLOCK, LANES), bf),
                   jax.ShapeDtypeStruct((batch, MOBA_HEADS, per_seq, LANES, MOBA_BLOCK), bf),
                   jax.ShapeDtypeStruct((batch, per_seq, MOBA_HEADS, LANES), jnp.float32)],
        compiler_params=_params(("parallel",)),
        name="rope_kmean",
    )(proj, proj, proj, tab_c, tab_a, tab_b)


HEADS_PER_STEP = 8


def _moba_kernel(qt_ref, kp_ref, vt_ref, km_ref, o_ref, qb_ref, p_ref, acc_ref, *, nb_pad):
    own = pl.program_id(2)
    half = MOBA_HEAD_DIM
    tq = MOBA_BLOCK
    bf = jnp.bfloat16
    blk = lax.broadcasted_iota(jnp.int32, (nb_pad, tq), 0)
    causal = lax.broadcasted_iota(jnp.int32, (tq, tq), 0) <= lax.broadcasted_iota(jnp.int32, (tq, tq), 1)

    def weights(h, j, m):
        kp = kp_ref[0, h, pl.ds(2 * j, 2)].reshape(2 * MOBA_BLOCK, LANES)
        s = jnp.dot(kp, qb_ref[h], preferred_element_type=jnp.float32)
        m_new = jnp.maximum(m, jnp.max(s, axis=0, keepdims=True))
        return m_new, jnp.exp(s - m_new).astype(bf), jnp.exp(m - m_new)

    def absorb(h, j, alpha):
        return (alpha * acc_ref[h]
                + jnp.dot(vt_ref[0, h, 2 * j], p_ref[h, :MOBA_BLOCK], preferred_element_type=jnp.float32)
                + jnp.dot(vt_ref[0, h, 2 * j + 1], p_ref[h, MOBA_BLOCK:], preferred_element_type=jnp.float32))

    carry = []
    for h in range(HEADS_PER_STEP):
        qt = qt_ref[0, h]
        gate = jnp.dot(km_ref[0, h], qt.astype(jnp.float32), precision=lax.Precision.HIGHEST,
                       preferred_element_type=jnp.float32)
        g = jnp.where(blk < own, gate, -jnp.inf)
        sel = jnp.zeros(g.shape, jnp.float32)
        for _ in range(MOBA_TOP_BLOCKS):
            m = jnp.max(g, axis=0, keepdims=True)
            idx = jnp.min(jnp.where(g == m, blk, nb_pad), axis=0, keepdims=True)
            hit = blk == idx
            sel = jnp.where(hit & (m > -jnp.inf), 1.0, sel)
            g = jnp.where(hit, -jnp.inf, g)
        bias = jnp.where(sel > 0.0, 0.0, MASK_VALUE).astype(bf)
        qb_ref[h] = jnp.concatenate([qt[:half], bias, jnp.zeros((LANES - half - nb_pad, tq), bf)], axis=0)

        s = jnp.dot(kp_ref[0, h, own], qt, preferred_element_type=jnp.float32)
        s = jnp.where(causal, s, MASK_VALUE)
        m = jnp.max(s, axis=0, keepdims=True)
        p = jnp.exp(s - m)
        acc_ref[h] = jnp.dot(vt_ref[0, h, own], p.astype(bf), preferred_element_type=jnp.float32)
        m, p_ref[h], alpha = weights(h, 0, m)
        carry += [m, alpha]

    def step(j, carry):
        new = []
        for h in range(HEADS_PER_STEP):
            m, alpha = carry[2 * h:2 * h + 2]
            acc_ref[h] = absorb(h, j - 1, alpha)
            m, p_ref[h], alpha = weights(h, j, m)
            new += [m, alpha]
        return tuple(new)

    n_pairs = jnp.maximum((own + 1) // 2, 1)
    carry = lax.fori_loop(1, n_pairs, step, tuple(carry))
    outs = []
    for h in range(HEADS_PER_STEP):
        acc = absorb(h, n_pairs - 1, carry[2 * h + 1])
        outs.append(acc[:half] / acc[half:half + 1])
    o_ref[...] = jnp.concatenate(outs, axis=0).T.astype(o_ref.dtype)


def moba_attention(qt, kp, vt, km, batch, seq):
    nq = seq // MOBA_BLOCK
    nb_pad = km.shape[2]
    hs = HEADS_PER_STEP
    return pl.pallas_call(
        functools.partial(_moba_kernel, nb_pad=nb_pad),
        grid=(batch, MOBA_HEADS // hs, nq),
        in_specs=[pl.BlockSpec((1, hs, LANES, MOBA_BLOCK), lambda b, hp, i: (b, hp, 0, i)),
                  pl.BlockSpec((1, hs, nq, MOBA_BLOCK, LANES), lambda b, hp, i: (b, hp, 0, 0, 0)),
                  pl.BlockSpec((1, hs, nq, LANES, MOBA_BLOCK), lambda b, hp, i: (b, hp, 0, 0, 0)),
                  pl.BlockSpec((1, hs, nb_pad, LANES), lambda b, hp, i: (b, hp, 0, 0))],
        out_specs=pl.BlockSpec((MOBA_BLOCK, hs * MOBA_HEAD_DIM), lambda b, hp, i: (b * nq + i, hp)),
        out_shape=jax.ShapeDtypeStruct((batch * seq, MOBA_WIDTH), jnp.bfloat16),
        scratch_shapes=[pltpu.VMEM((hs, LANES, MOBA_BLOCK), jnp.bfloat16),
                        pltpu.VMEM((hs, 2 * MOBA_BLOCK, MOBA_BLOCK), jnp.bfloat16),
                        pltpu.VMEM((hs, LANES, MOBA_BLOCK), jnp.float32)],
        compiler_params=_params(("parallel", "parallel", "arbitrary")),
        name="moba_attention",
    )(qt, kp, vt, km)


def _gla_kernel(q_ref, k_ref, v_ref, r_ref, cum_ref, gain_ref, o_ref, state_ref, *, rows):
    @pl.when(pl.program_id(1) == 0)
    def _():
        state_ref[...] = jnp.zeros_like(state_ref)

    c = GLA_CHUNK
    r_i = lax.broadcasted_iota(jnp.int32, (c, c), 0)
    c_i = lax.broadcasted_iota(jnp.int32, (c, c), 1)
    lower = r_i >= c_i
    nt = (((1,), (1,)), ((), ()))
    tn = (((0,), (0,)), ((), ()))
    bf = jnp.bfloat16
    f32 = jnp.float32
    for j in range(rows // c):
        sl = slice(j * c, (j + 1) * c)
        for h in range(GLA_HEADS):
            hl = slice(h * LANES, (h + 1) * LANES)
            cum = cum_ref[sl, hl]
            cum_last = cum[c - 1:c, :]
            q = q_ref[sl, hl].astype(f32) * (GLA_KEY_DIM ** -0.5)
            k = k_ref[sl, hl].astype(f32)
            v = v_ref[sl, hl]
            q_dec = (q * jnp.exp(cum)).astype(bf)
            k_inv = (k * jnp.exp(-cum)).astype(bf)
            k_end = (k * jnp.exp(cum_last - cum)).astype(bf)
            att = lax.dot_general(q_dec, k_inv, nt, preferred_element_type=f32)
            att = jnp.where(lower, att, 0.0)
            intra = jnp.dot(att.astype(bf), v, preferred_element_type=f32)
            state = state_ref[h]
            inter = lax.dot_general(q_dec, state.astype(bf), nt, preferred_element_type=f32)
            d_state = lax.dot_general(v, k_end, tn, preferred_element_type=f32)
            state_ref[h] = state * jnp.exp(cum_last) + d_state
            o = intra + inter
            mu = jnp.mean(o, axis=-1, keepdims=True)
            var = jnp.mean(jnp.square(o - mu), axis=-1, keepdims=True)
            on = (o - mu) * lax.rsqrt(var + LN_EPS) * gain_ref[:, hl]
            rg = r_ref[sl, hl].astype(f32)
            o_ref[sl, hl] = (on * (rg * jax.nn.sigmoid(rg))).astype(o_ref.dtype)


def gla(proj, cum, gain, batch, seq, rows):
    t = proj.shape[0]
    assert seq % rows == 0 and rows % GLA_CHUNK == 0
    ns = seq // rows
    col = lambda c0: pl.BlockSpec((rows, GLA_WIDTH), lambda b, i: (b * ns + i, c0 // GLA_WIDTH))
    return pl.pallas_call(
        functools.partial(_gla_kernel, rows=rows),
        grid=(batch, ns),
        in_specs=[col(COL_QG), col(COL_KG), col(COL_VG), col(COL_RG), col(0),
                  pl.BlockSpec((1, GLA_WIDTH), lambda b, i: (0, 0))],
        out_specs=col(0),
        out_shape=jax.ShapeDtypeStruct((t, GLA_WIDTH), jnp.bfloat16),
        scratch_shapes=[pltpu.VMEM((GLA_HEADS, LANES, LANES), jnp.float32)],
        compiler_params=_params(("parallel", "arbitrary")),
        name="gla",
    )(proj, proj, proj, proj, cum, gain)


def _layer_norm(z, g, b):
    mu = jnp.mean(z, axis=-1, keepdims=True)
    var = jnp.mean(jnp.square(z - mu), axis=-1, keepdims=True)
    return (z - mu) * lax.rsqrt(var + LN_EPS) * g + b


def _merge_kernel(ya_ref, yb_ref, ga_ref, gb_ref, x_ref, wa_ref, wb_ref, wo_ref, g_ref, b_ref, xo_ref, xb_ref):
    bf = jnp.bfloat16
    f32 = jnp.float32
    pa = jnp.dot(ya_ref[...], wa_ref[...], preferred_element_type=f32)
    pb = jnp.dot(yb_ref[...], wb_ref[...], preferred_element_type=f32)
    merged = jax.nn.sigmoid(ga_ref[...].astype(f32)) * pa + jax.nn.sigmoid(gb_ref[...].astype(f32)) * pb
    mix = jnp.dot(merged.astype(bf), wo_ref[...], preferred_element_type=jnp.float32)
    y = _layer_norm(DEEPNORM_ALPHA * x_ref[...] + mix, g_ref[...], b_ref[...])
    xo_ref[...] = y
    xb_ref[...] = y.astype(bf)


def merge_out(ya, yb, proj, x, wa, wb, wo, g, b, tm):
    t = x.shape[0]
    row = lambda w, cb=0: pl.BlockSpec((tm, w), lambda i: (i, cb))
    const = lambda r, c: pl.BlockSpec((r, c), lambda i: (0, 0))
    return pl.pallas_call(
        _merge_kernel,
        grid=(t // tm,),
        in_specs=[row(MOBA_WIDTH), row(GLA_WIDTH), row(D_MODEL, COL_GATE_A // D_MODEL), row(D_MODEL, COL_GATE_B // D_MODEL),
                  row(D_MODEL), const(MOBA_WIDTH, D_MODEL), const(GLA_WIDTH, D_MODEL), const(D_MODEL, D_MODEL),
                  const(1, D_MODEL), const(1, D_MODEL)],
        out_specs=[row(D_MODEL), row(D_MODEL)],
        out_shape=[jax.ShapeDtypeStruct((t, D_MODEL), jnp.float32), jax.ShapeDtypeStruct((t, D_MODEL), jnp.bfloat16)],
        compiler_params=_params(("parallel",)),
        name="merge_out_ln",
    )(ya, yb, proj, proj, x, wa, wb, wo, g, b)


def _ffn_kernel(xb_ref, x_ref, wg_ref, wu_ref, wd_ref, g_ref, b_ref, xo_ref, xbo_ref, acc_ref):
    f = pl.program_id(1)

    @pl.when(f == 0)
    def _():
        acc_ref[...] = jnp.zeros_like(acc_ref)

    h = xb_ref[...]
    gate = jnp.dot(h, wg_ref[...], preferred_element_type=jnp.float32)
    up = jnp.dot(h, wu_ref[...], preferred_element_type=jnp.float32)
    hid = gate * jax.nn.sigmoid(gate) * up
    acc_ref[...] += jnp.dot(hid.astype(jnp.bfloat16), wd_ref[...], preferred_element_type=jnp.float32)

    @pl.when(f == pl.num_programs(1) - 1)
    def _():
        y = _layer_norm(DEEPNORM_ALPHA * x_ref[...] + acc_ref[...], g_ref[...], b_ref[...])
        xo_ref[...] = y
        xbo_ref[...] = y.astype(jnp.bfloat16)


def ffn(xb, x, wg, wu, wd, g, b, tm, tf):
    t = x.shape[0]
    fdim = wg.shape[1]
    row = lambda w: pl.BlockSpec((tm, w), lambda i, f: (i, 0))
    const = pl.BlockSpec((1, D_MODEL), lambda i, f: (0, 0))
    return pl.pallas_call(
        _ffn_kernel,
        grid=(t // tm, fdim // tf),
        in_specs=[row(D_MODEL), row(D_MODEL),
                  pl.BlockSpec((D_MODEL, tf), lambda i, f: (0, f)),
                  pl.BlockSpec((D_MODEL, tf), lambda i, f: (0, f)),
                  pl.BlockSpec((tf, D_MODEL), lambda i, f: (f, 0)),
                  const, const],
        out_specs=[row(D_MODEL), row(D_MODEL)],
        out_shape=[jax.ShapeDtypeStruct((t, D_MODEL), jnp.float32), jax.ShapeDtypeStruct((t, D_MODEL), jnp.bfloat16)],
        scratch_shapes=[pltpu.VMEM((tm, D_MODEL), jnp.float32)],
        compiler_params=_params(("parallel", "arbitrary")),
        name="swiglu_ln",
    )(xb, x, wg, wu, wd, g, b)


ROUTE_E1, ROUTE_E2, ROUTE_W1, ROUTE_W2 = 0, 1, 2, 3


def _router_kernel(x_ref, w_ref, o_ref):
    logits = jnp.dot(x_ref[...], w_ref[...], precision=lax.Precision.HIGHEST, preferred_element_type=jnp.float32)
    lane = lax.broadcasted_iota(jnp.int32, (1, LANES), 1)
    g = jnp.where(lane < N_EXPERTS, logits, -jnp.inf)
    m1 = jnp.max(g, axis=-1, keepdims=True)
    i1 = jnp.min(jnp.where(g == m1, lane, LANES), axis=-1, keepdims=True)
    g2 = jnp.where(lane == i1, -jnp.inf, g)
    m2 = jnp.max(g2, axis=-1, keepdims=True)
    i2 = jnp.min(jnp.where(g2 == m2, lane, LANES), axis=-1, keepdims=True)
    e2 = jnp.exp(m2 - m1)
    w1 = 1.0 / (1.0 + e2)
    w2 = e2 / (1.0 + e2)
    out = jnp.where(lane == ROUTE_E1, i1.astype(jnp.float32), 0.0)
    out = jnp.where(lane == ROUTE_E2, i2.astype(jnp.float32), out)
    out = jnp.where(lane == ROUTE_W1, w1, out)
    o_ref[...] = jnp.where(lane == ROUTE_W2, w2, out)


def router(x, w, tm):
    t = x.shape[0]
    return pl.pallas_call(
        _router_kernel,
        grid=(t // tm,),
        in_specs=[pl.BlockSpec((tm, D_MODEL), lambda i: (i, 0)), pl.BlockSpec((D_MODEL, LANES), lambda i: (0, 0))],
        out_specs=pl.BlockSpec((tm, LANES), lambda i: (i, 0)),
        out_shape=jax.ShapeDtypeStruct((t, LANES), jnp.float32),
        compiler_params=_params(("parallel",)),
        name="router",
    )(x, w)


def _routing_tables(route, tm):
    t = route.shape[0]
    experts = route[:, ROUTE_E1:ROUTE_E2 + 1].astype(jnp.int32).reshape(-1)
    onehot = (experts[:, None] == jnp.arange(N_EXPERTS, dtype=jnp.int32)[None, :]).astype(jnp.int32)
    csum = jnp.cumsum(onehot, axis=0)
    rank = jnp.sum((csum - onehot) * onehot, axis=1)
    size = -(-csum[-1] // tm) * tm
    end = jnp.cumsum(size)
    pos = jnp.sum(onehot * (end - size)[None, :], axis=1) + rank
    n_tiles = 2 * t // tm + N_EXPERTS
    tok = jnp.zeros(((n_tiles + 1) * tm,), jnp.int32).at[pos].set(
        jnp.arange(2 * t, dtype=jnp.int32) // 2, unique_indices=True, mode="promise_in_bounds")
    tile_start = jnp.arange(n_tiles, dtype=jnp.int32) * tm
    tile_expert = jnp.minimum(jnp.sum((tile_start[:, None] >= end[None, :]).astype(jnp.int32), axis=1), N_EXPERTS - 1)
    n_used = end[-1] // tm
    return tok.reshape(n_tiles + 1, tm), pos.reshape(t, 2), tile_expert, n_used.reshape(1)


SUBLANES = 8


def _issue_row_copies(slot, n_rows, idx_ref, src_hbm, dst_ref, sem):
    for s in range(2):
        @pl.when(slot == s)
        def _():
            def body(j, c):
                base = pl.multiple_of(j * SUBLANES, SUBLANES)
                for u in range(SUBLANES):
                    dst = (dst_ref.at[s, j, pl.ds(u, 1), :] if len(dst_ref.shape) == 4
                           else dst_ref.at[s, pl.ds(base + u, 1), :])
                    pltpu.make_async_copy(src_hbm.at[pl.ds(idx_ref[s, base + u], 1), :], dst, sem.at[s]).start()
                return c
            lax.fori_loop(0, n_rows // SUBLANES, body, 0)


def _moe_kernel(te_ref, nu_ref, tok_hbm, x_hbm, wg_ref, wu_ref, wd_ref, y_ref,
                idx_ref, xg_ref, xb_ref, acc_ref, idx_sem, row_sem, *, tm):
    i = pl.program_id(0)
    f = pl.program_id(1)
    nt = pl.num_programs(0)
    nf = pl.num_programs(1)
    slot = i % 2
    nxt = 1 - slot
    chunk = tm // (nf + 1)

    def idx_copy(tile, s):
        return pltpu.make_async_copy(tok_hbm.at[tile], idx_ref.at[s], idx_sem.at[s])

    def rows_done(s):
        return pltpu.make_async_copy(xg_ref.at[s], xg_ref.at[s], row_sem.at[s])

    def issue_chunk():
        first = (f + 1) * chunk
        for u in range(chunk):
            dst = xg_ref.at[nxt, first // SUBLANES + u // SUBLANES, pl.ds(u % SUBLANES, 1), :]
            pltpu.make_async_copy(x_hbm.at[pl.ds(idx_ref[nxt, first + u], 1), :], dst, row_sem.at[nxt]).start()

    @pl.when(f == 0)
    def _():
        @pl.when(i == 0)
        def _():
            idx_copy(0, 0).start()
            idx_copy(0, 0).wait()
            _issue_row_copies(0, tm, idx_ref, x_hbm, xg_ref, row_sem)
            idx_copy(1, 1).start()

        idx_copy(i + 1, nxt).wait()
        _issue_row_copies(nxt, chunk, idx_ref, x_hbm, xg_ref, row_sem)
        rows_done(slot).wait()
        xb_ref[...] = xg_ref[slot].reshape(tm, D_MODEL).astype(jnp.bfloat16)
        acc_ref[...] = jnp.zeros_like(acc_ref)

    @pl.when((f == 1) & (i + 1 < nt))
    def _():
        idx_copy(i + 2, slot).start()

    valid = i < nu_ref[0]

    @pl.when(valid)
    def _():
        issue_chunk()
        h = xb_ref[...]
        bf = jnp.bfloat16
        gate = jnp.dot(h, wg_ref[0].astype(bf), preferred_element_type=jnp.float32)
        up = jnp.dot(h, wu_ref[0].astype(bf), preferred_element_type=jnp.float32)
        hid = gate * jax.nn.sigmoid(gate) * up
        acc_ref[...] += jnp.dot(hid.astype(bf), wd_ref[0].astype(bf), preferred_element_type=jnp.float32)

    @pl.when(jnp.logical_not(valid))
    def _():
        issue_chunk()

    @pl.when(f == nf - 1)
    def _():
        y_ref[...] = acc_ref[...]

        @pl.when(i == nt - 1)
        def _():
            rows_done(nxt).wait()


def moe_experts(tile_expert, n_used, tok, x, wg, wu, wd, tm, tf):
    n_tiles = tok.shape[0] - 1
    fdim = wg.shape[2]
    assert fdim // tf >= 2 and n_tiles >= 2 and tm % (SUBLANES * (fdim // tf + 1)) == 0
    grid_spec = pltpu.PrefetchScalarGridSpec(
        num_scalar_prefetch=2,
        grid=(n_tiles, fdim // tf),
        in_specs=[pl.BlockSpec(memory_space=pl.ANY), pl.BlockSpec(memory_space=pl.ANY),
                  pl.BlockSpec((1, D_MODEL, tf), lambda i, f, te, nu: (te[i], 0, f)),
                  pl.BlockSpec((1, D_MODEL, tf), lambda i, f, te, nu: (te[i], 0, f)),
                  pl.BlockSpec((1, tf, D_MODEL), lambda i, f, te, nu: (te[i], f, 0))],
        out_specs=pl.BlockSpec((tm, D_MODEL), lambda i, f, te, nu: (i, 0)),
        scratch_shapes=[pltpu.SMEM((2, tm), jnp.int32), pltpu.VMEM((2, tm // SUBLANES, SUBLANES, D_MODEL), jnp.float32),
                        pltpu.VMEM((tm, D_MODEL), jnp.bfloat16), pltpu.VMEM((tm, D_MODEL), jnp.float32),
                        pltpu.SemaphoreType.DMA((2,)), pltpu.SemaphoreType.DMA((2,))])
    return pl.pallas_call(
        functools.partial(_moe_kernel, tm=tm),
        grid_spec=grid_spec,
        out_shape=jax.ShapeDtypeStruct((n_tiles * tm, D_MODEL), jnp.float32),
        compiler_params=_params(("arbitrary", "arbitrary"), disable_bounds_checks=True),
        name="moe_experts",
    )(tile_expert, n_used, tok, x, wg, wu, wd)


def _combine_kernel(pos_hbm, y_hbm, route_ref, x_ref, g_ref, b_ref, xo_ref, idx_ref, rows_ref, idx_sem, row_sem, *, tc):
    i = pl.program_id(0)
    nt = pl.num_programs(0)
    slot = i % 2

    def idx_copy(tile, s):
        return pltpu.make_async_copy(pos_hbm.at[tile], idx_ref.at[s], idx_sem.at[s])

    def gather_rows(s):
        _issue_row_copies(s, 2 * tc, idx_ref, y_hbm, rows_ref, row_sem)

    @pl.when(i == 0)
    def _():
        idx_copy(0, 0).start()
        idx_copy(0, 0).wait()
        gather_rows(0)
        idx_copy(1, 1).start()

    @pl.when(i + 1 < nt)
    def _():
        idx_copy(i + 1, 1 - slot).wait()
        gather_rows(1 - slot)

    pltpu.make_async_copy(rows_ref.at[slot], rows_ref.at[slot], row_sem.at[slot]).wait()

    @pl.when(i + 2 < nt)
    def _():
        idx_copy(i + 2, slot).start()

    route = route_ref[...]
    w1 = route[:, ROUTE_W1:ROUTE_W1 + 1]
    w2 = route[:, ROUTE_W2:ROUTE_W2 + 1]
    groups = tc // SUBLANES
    f = w1 * rows_ref[slot, :groups].reshape(tc, D_MODEL) + w2 * rows_ref[slot, groups:].reshape(tc, D_MODEL)
    xo_ref[...] = _layer_norm(DEEPNORM_ALPHA * x_ref[...] + f, g_ref[...], b_ref[...])


def moe_combine(pos, y, route, x, g, b, tc):
    t = x.shape[0]
    n_tiles = t // tc
    assert n_tiles >= 2
    pos_tiles = pos.reshape(n_tiles, tc, 2).transpose(0, 2, 1).reshape(n_tiles, 2 * tc)
    const = pl.BlockSpec((1, D_MODEL), lambda i: (0, 0))
    return pl.pallas_call(
        functools.partial(_combine_kernel, tc=tc),
        grid=(n_tiles,),
        in_specs=[pl.BlockSpec(memory_space=pl.ANY), pl.BlockSpec(memory_space=pl.ANY),
                  pl.BlockSpec((tc, LANES), lambda i: (i, 0)), pl.BlockSpec((tc, D_MODEL), lambda i: (i, 0)),
                  const, const],
        out_specs=pl.BlockSpec((tc, D_MODEL), lambda i: (i, 0)),
        out_shape=jax.ShapeDtypeStruct((t, D_MODEL), jnp.float32),
        scratch_shapes=[pltpu.SMEM((2, 2 * tc), jnp.int32),
                        pltpu.VMEM((2, 2 * tc // SUBLANES, SUBLANES, D_MODEL), jnp.float32),
                        pltpu.SemaphoreType.DMA((2,)), pltpu.SemaphoreType.DMA((2,))],
        compiler_params=_params(("arbitrary",), disable_bounds_checks=True),
        name="moe_combine_ln",
    )(pos_tiles, y, route, x, g, b)


def _rope_tables(seq):
    half = ROPE_DIM // 2
    inv_freq = jnp.power(ROPE_THETA, -jnp.arange(0, ROPE_DIM, 2, dtype=jnp.float32) / ROPE_DIM)
    ang = jnp.arange(seq, dtype=jnp.float32)[:, None] * inv_freq[None, :]
    cos, sin = jnp.cos(ang), jnp.sin(ang)
    ones = jnp.ones((seq, MOBA_HEAD_DIM - ROPE_DIM), jnp.float32)
    zeros = jnp.zeros((seq, MOBA_HEAD_DIM - ROPE_DIM), jnp.float32)
    z8 = jnp.zeros((seq, half), jnp.float32)
    tab_c = jnp.concatenate([cos, cos, ones], axis=1)
    tab_a = jnp.concatenate([-sin, z8, zeros], axis=1)
    tab_b = jnp.concatenate([z8, sin, zeros], axis=1)
    dup = lambda t: jnp.concatenate([t, t], axis=1)
    return dup(tab_c), dup(tab_a), dup(tab_b)


def _split_w_in(w):
    offs = np.cumsum([0, 512, 512, 512, 512, 512, 512, 512, GLA_GATE_RANK, 1024, 1024])
    part = lambda j: w[:, int(offs[j]):int(offs[j + 1])]
    qa, ka, va, qg, kg, vg, rg, lg, gate_a, gate_b = [part(j) for j in range(10)]
    main = jnp.concatenate([gate_a, gate_b, qa, ka, va, qg, kg, vg, rg], axis=1).astype(jnp.bfloat16)
    lg = jnp.pad(lg, ((0, 0), (0, LANES - GLA_GATE_RANK))).astype(jnp.bfloat16)
    return main, lg


def kernel(x, w_in, w_gla_gate_up, b_gla_gate, gla_norm_g, w_branch_a, w_branch_b, w_out, ln_mix_g, ln_mix_b,
           ffn_w_gate, ffn_w_up, ffn_w_down, moe_w_router, moe_w_gate, moe_w_up, moe_w_down, ln_ffn_g, ln_ffn_b):
    batch, seq, d = x.shape
    t = batch * seq
    bf16_rows = 16
    nb_pad = -(-(seq // MOBA_BLOCK) // bf16_rows) * bf16_rows
    assert d == D_MODEL and seq % MOBA_BLOCK == 0 and nb_pad <= LANES - MOBA_HEAD_DIM
    bf = jnp.bfloat16
    tab_c, tab_a, tab_b = _rope_tables(seq)
    xf = x.reshape(t, d)
    xb = xf
    row_tile = math.gcd(t, 1024)
    for layer in range(DEPTH):
        w_main, w_lg = _split_w_in(w_in[layer])
        w_up = jnp.pad(w_gla_gate_up[layer], ((0, LANES - GLA_GATE_RANK), (0, 0)))
        proj, cum = in_proj(xb, w_main, w_lg, w_up, b_gla_gate[layer].reshape(1, -1), math.gcd(t, 2048), 512)
        qt, kp, vt, kmean = rope_prepass(proj, tab_c, tab_a, tab_b, batch, seq)
        km = jnp.pad(kmean.transpose(0, 2, 1, 3), ((0, 0), (0, 0), (0, nb_pad - seq // MOBA_BLOCK), (0, 0)))
        y_a = moba_attention(qt, kp, vt, km, batch, seq)
        y_b = gla(proj, cum, gla_norm_g[layer].reshape(1, -1), batch, seq, 512)
        xf, xb = merge_out(y_a, y_b, proj, xf, w_branch_a[layer].astype(bf), w_branch_b[layer].astype(bf),
                           w_out[layer].astype(bf), ln_mix_g[layer].reshape(1, -1), ln_mix_b[layer].reshape(1, -1),
                           math.gcd(t, 512))
        j = layer // 2
        ln_g, ln_b = ln_ffn_g[layer].reshape(1, -1), ln_ffn_b[layer].reshape(1, -1)
        if layer % 2 == 0:
            xf, xb = ffn(xb, xf, ffn_w_gate[j].astype(bf), ffn_w_up[j].astype(bf), ffn_w_down[j].astype(bf),
                         ln_g, ln_b, row_tile, 512)
        else:
            w_r = jnp.pad(moe_w_router[j], ((0, 0), (0, LANES - N_EXPERTS)))
            route = router(xf, w_r, math.gcd(t, 512))
            tok, pos, tile_expert, n_used = _routing_tables(route, row_tile)
            y = moe_experts(tile_expert, n_used, tok, xf, moe_w_gate[j], moe_w_up[j], moe_w_down[j], row_tile, 512)
            xf = moe_combine(pos, y, route, xf, ln_g, ln_b, math.gcd(t, 512))
    return xf.reshape(batch, seq, d)
```

```python
import functools
import math

import jax
import jax.numpy as jnp
import numpy as np
from jax import lax
from jax.experimental import pallas as pl
from jax.experimental.pallas import tpu as pltpu

D_MODEL = 1024
DEPTH = 2
MOBA_HEADS = 8
MOBA_HEAD_DIM = 64
MOBA_WIDTH = 512
MOBA_BLOCK = 256
MOBA_TOP_BLOCKS = 3
ROPE_THETA = 500000.0
ROPE_DIM = 16
GLA_HEADS = 4
GLA_KEY_DIM = 128
GLA_WIDTH = 512
GLA_GATE_RANK = 16
GLA_GATE_TAU = 16.0
GLA_CHUNK = 64
FFN_DIM = 3584
N_EXPERTS = 8
DEEPNORM_ALPHA = (2 * DEPTH) ** 0.25
LN_EPS = 1e-5
MASK_VALUE = -1e30

LANES = 128
VMEM_LIMIT = 56 * 1024 * 1024

COL_GATE_A, COL_GATE_B = 0, 1024
COL_QA, COL_KA, COL_VA = 2048, 2560, 3072
COL_QG, COL_KG, COL_VG, COL_RG = 3584, 4096, 4608, 5120
MAIN_COLS = 5632


def _params(sem, **kw):
    return pltpu.CompilerParams(dimension_semantics=sem, vmem_limit_bytes=VMEM_LIMIT, **kw)


CUM_BLOCK = 256


def _split_bf16(a):
    f32, bf = jnp.float32, jnp.bfloat16
    hi = a.astype(bf)
    r = a - hi.astype(f32)
    mid = r.astype(bf)
    return hi, mid, (r - mid.astype(f32)).astype(bf)


def _in_proj_kernel(x_ref, w_ref, wlg_ref, wup_ref, b_ref, o_ref, cum_ref):
    x = x_ref[...].astype(jnp.bfloat16)
    o_ref[...] = jnp.dot(x, w_ref[...], preferred_element_type=jnp.float32).astype(o_ref.dtype)

    @pl.when(pl.program_id(1) == 0)
    def _():
        f32 = jnp.float32
        dot = functools.partial(jnp.dot, preferred_element_type=f32)
        lg_hi, lg_mid, _ = _split_bf16(dot(x, wlg_ref[...]))
        w_hi, w_mid, _ = _split_bf16(wup_ref[...])
        z = dot(lg_hi, w_hi) + (dot(lg_hi, w_mid) + dot(lg_mid, w_hi)) + b_ref[...]
        g = (jnp.minimum(z, 0.0) - jnp.log(1.0 + jnp.exp(-jnp.abs(z)))) * (1.0 / GLA_GATE_TAU)
        r_i = lax.broadcasted_iota(jnp.int32, (CUM_BLOCK, CUM_BLOCK), 0)
        c_i = lax.broadcasted_iota(jnp.int32, (CUM_BLOCK, CUM_BLOCK), 1)
        tri = jnp.where((r_i >= c_i) & (r_i // GLA_CHUNK == c_i // GLA_CHUNK), 1.0, 0.0).astype(jnp.bfloat16)
        for c in range(x.shape[0] // CUM_BLOCK):
            sl = slice(c * CUM_BLOCK, (c + 1) * CUM_BLOCK)
            g_hi, g_mid, g_lo = _split_bf16(g[sl, :])
            cum_ref[sl, :] = dot(tri, g_hi) + (dot(tri, g_mid) + dot(tri, g_lo))


def in_proj(x, w, w_lg, w_up, b, tm, tn):
    m, k = x.shape
    n = w.shape[1]
    assert m % tm == 0 and n % tn == 0 and tm % CUM_BLOCK == 0 and CUM_BLOCK % GLA_CHUNK == 0
    const = lambda r, c: pl.BlockSpec((r, c), lambda i, j: (0, 0))
    return pl.pallas_call(
        _in_proj_kernel,
        grid=(m // tm, n // tn),
        in_specs=[pl.BlockSpec((tm, k), lambda i, j: (i, 0)),
                  pl.BlockSpec((k, tn), lambda i, j: (0, j)),
                  const(k, LANES), const(LANES, GLA_WIDTH), const(1, GLA_WIDTH)],
        out_specs=[pl.BlockSpec((tm, tn), lambda i, j: (i, j)),
                   pl.BlockSpec((tm, GLA_WIDTH), lambda i, j: (i, 0))],
        out_shape=[jax.ShapeDtypeStruct((m, n), jnp.bfloat16), jax.ShapeDtypeStruct((m, GLA_WIDTH), jnp.float32)],
        compiler_params=_params(("parallel", "arbitrary")),
        name="in_proj",
    )(x, w, w_lg, w_up, b)


def _rope_kernel(q_ref, k_ref, v_ref, c_ref, a_ref, b_ref, qt_ref, kp_ref, vt_ref, km_ref, *, per_seq):
    half = MOBA_HEAD_DIM
    blk = pl.program_id(0) % per_seq
    lane = lax.broadcasted_iota(jnp.int32, (1, LANES), 1)
    lower = lane < half
    ind = jnp.where(lane == half + blk, 1.0, 0.0)
    zeros = jnp.zeros((LANES - half, MOBA_BLOCK), jnp.float32)
    ones = jnp.ones((LANES - half, MOBA_BLOCK), jnp.float32)

    def rot(t):
        return (t * c_ref[...] + pltpu.roll(t, LANES - ROPE_DIM // 2, 1) * a_ref[...]
                + pltpu.roll(t, ROPE_DIM // 2, 1) * b_ref[...])

    for hp in range(MOBA_WIDTH // LANES):
        sl = slice(hp * LANES, (hp + 1) * LANES)
        f32 = jnp.float32
        q_t = (rot(q_ref[:, sl].astype(f32)) * (MOBA_HEAD_DIM ** -0.5)).T
        v_t = v_ref[:, sl].astype(f32).T
        kr = rot(k_ref[:, sl].astype(f32))
        km = jnp.sum(kr, axis=0, keepdims=True) * (1.0 / MOBA_BLOCK)
        for j, (kh, kmh) in enumerate(((kr, km), (pltpu.roll(kr, half, 1), pltpu.roll(km, half, 1)))):
            h = 2 * hp + j
            rows = slice(j * half, (j + 1) * half)
            qt_ref[0, h] = jnp.concatenate([q_t[rows], zeros], axis=0).astype(qt_ref.dtype)
            vt_ref[0, h, 0] = jnp.concatenate([v_t[rows], ones], axis=0).astype(vt_ref.dtype)
            kp_ref[0, h, 0] = jnp.where(lower, kh, ind).astype(kp_ref.dtype)
            km_ref[0, 0, h:h + 1, :] = jnp.where(lower, kmh, 0.0)


def rope_prepass(proj, tab_c, tab_a, tab_b, batch, seq):
    per_seq = seq // MOBA_BLOCK
    row = lambda cb: pl.BlockSpec((MOBA_BLOCK, MOBA_WIDTH), lambda i: (i, cb))
    tab = pl.BlockSpec((MOBA_BLOCK, LANES), lambda i: (i % per_seq, 0))
    bf = jnp.bfloat16
    return pl.pallas_call(
        functools.partial(_rope_kernel, per_seq=per_seq),
        grid=(batch * per_seq,),
        in_specs=[row(COL_QA // MOBA_WIDTH), row(COL_KA // MOBA_WIDTH), row(COL_VA // MOBA_WIDTH), tab, tab, tab],
        out_specs=[pl.BlockSpec((1, MOBA_HEADS, LANES, MOBA_BLOCK), lambda i: (i // per_seq, 0, 0, i % per_seq)),
                   pl.BlockSpec((1, MOBA_HEADS, 1, MOBA_BLOCK, LANES), lambda i: (i // per_seq, 0, i % per_seq, 0, 0)),
                   pl.BlockSpec((1, MOBA_HEADS, 1, LANES, MOBA_BLOCK), lambda i: (i // per_seq, 0, i % per_seq, 0, 0)),
                   pl.BlockSpec((1, 1, MOBA_HEADS, LANES), lambda i: (i // per_seq, i % per_seq, 0, 0))],
        out_shape=[jax.ShapeDtypeStruct((batch, MOBA_HEADS, LANES, seq), bf),
                   jax.ShapeDtypeStruct((batch, MOBA_HEADS, per_seq, MOBA_BLOCK, LANES), bf),
                   jax.ShapeDtypeStruct((batch, MOBA_HEADS, per_seq, LANES, MOBA_BLOCK), bf),
                   jax.ShapeDtypeStruct((batch, per_seq, MOBA_HEADS, LANES), jnp.float32)],
        compiler_params=_params(("parallel",)),
        name="rope_kmean",
    )(proj, proj, proj, tab_c, tab_a, tab_b)


HEADS_PER_STEP = 8


def _moba_kernel(qt_ref, kp_ref, vt_ref, km_ref, o_ref, qb_ref, p_ref, acc_ref, *, nb_pad):
    own = pl.program_id(2)
    half = MOBA_HEAD_DIM
    tq = MOBA_BLOCK
    bf = jnp.bfloat16
    blk = lax.broadcasted_iota(jnp.int32, (nb_pad, tq), 0)
    causal = lax.broadcasted_iota(jnp.int32, (tq, tq), 0) <= lax.broadcasted_iota(jnp.int32, (tq, tq), 1)

    def weights(h, j, m):
        kp = kp_ref[0, h, pl.ds(2 * j, 2)].reshape(2 * MOBA_BLOCK, LANES)
        s = jnp.dot(kp, qb_ref[h], preferred_element_type=jnp.float32)
        m_new = jnp.maximum(m, jnp.max(s, axis=0, keepdims=True))
        return m_new, jnp.exp(s - m_new).astype(bf), jnp.exp(m - m_new)

    def absorb(h, j, alpha):
        return (alpha * acc_ref[h]
                + jnp.dot(vt_ref[0, h, 2 * j], p_ref[h, :MOBA_BLOCK], preferred_element_type=jnp.float32)
                + jnp.dot(vt_ref[0, h, 2 * j + 1], p_ref[h, MOBA_BLOCK:], preferred_element_type=jnp.float32))

    carry = []
    for h in range(HEADS_PER_STEP):
        qt = qt_ref[0, h]
        gate = jnp.dot(km_ref[0, h], qt.astype(jnp.float32), precision=lax.Precision.HIGHEST,
                       preferred_element_type=jnp.float32)
        g = jnp.where(blk < own, gate, -jnp.inf)
        sel = jnp.zeros(g.shape, jnp.float32)
        for _ in range(MOBA_TOP_BLOCKS):
            m = jnp.max(g, axis=0, keepdims=True)
            idx = jnp.min(jnp.where(g == m, blk, nb_pad), axis=0, keepdims=True)
            hit = blk == idx
            sel = jnp.where(hit & (m > -jnp.inf), 1.0, sel)
            g = jnp.where(hit, -jnp.inf, g)
        bias = jnp.where(sel > 0.0, 0.0, MASK_VALUE).astype(bf)
        qb_ref[h] = jnp.concatenate([qt[:half], bias, jnp.zeros((LANES - half - nb_pad, tq), bf)], axis=0)

        s = jnp.dot(kp_ref[0, h, own], qt, preferred_element_type=jnp.float32)
        s = jnp.where(causal, s, MASK_VALUE)
        m = jnp.max(s, axis=0, keepdims=True)
        p = jnp.exp(s - m)
        acc_ref[h] = jnp.dot(vt_ref[0, h, own], p.astype(bf), preferred_element_type=jnp.float32)
        m, p_ref[h], alpha = weights(h, 0, m)
        carry += [m, alpha]

    def step(j, carry):
        new = []
        for h in range(HEADS_PER_STEP):
            m, alpha = carry[2 * h:2 * h + 2]
            acc_ref[h] = absorb(h, j - 1, alpha)
            m, p_ref[h], alpha = weights(h, j, m)
            new += [m, alpha]
        return tuple(new)

    n_pairs = jnp.maximum((own + 1) // 2, 1)
    carry = lax.fori_loop(1, n_pairs, step, tuple(carry))
    outs = []
    for h in range(HEADS_PER_STEP):
        acc = absorb(h, n_pairs - 1, carry[2 * h + 1])
        outs.append(acc[:half] / acc[half:half + 1])
    o_ref[...] = jnp.concatenate(outs, axis=0).T.astype(o_ref.dtype)


def moba_attention(qt, kp, vt, km, batch, seq):
    nq = seq // MOBA_BLOCK
    nb_pad = km.shape[2]
    hs = HEADS_PER_STEP
    return pl.pallas_call(
        functools.partial(_moba_kernel, nb_pad=nb_pad),
        grid=(batch, MOBA_HEADS // hs, nq),
        in_specs=[pl.BlockSpec((1, hs, LANES, MOBA_BLOCK), lambda b, hp, i: (b, hp, 0, i)),
                  pl.BlockSpec((1, hs, nq, MOBA_BLOCK, LANES), lambda b, hp, i: (b, hp, 0, 0, 0)),
                  pl.BlockSpec((1, hs, nq, LANES, MOBA_BLOCK), lambda b, hp, i: (b, hp, 0, 0, 0)),
                  pl.BlockSpec((1, hs, nb_pad, LANES), lambda b, hp, i: (b, hp, 0, 0))],
        out_specs=pl.BlockSpec((MOBA_BLOCK, hs * MOBA_HEAD_DIM), lambda b, hp, i: (b * nq + i, hp)),
        out_shape=jax.ShapeDtypeStruct((batch * seq, MOBA_WIDTH), jnp.bfloat16),
        scratch_shapes=[pltpu.VMEM((hs, LANES, MOBA_BLOCK), jnp.bfloat16),
                        pltpu.VMEM((hs, 2 * MOBA_BLOCK, MOBA_BLOCK), jnp.bfloat16),
                        pltpu.VMEM((hs, LANES, MOBA_BLOCK), jnp.float32)],
        compiler_params=_params(("parallel", "parallel", "arbitrary")),
        name="moba_attention",
    )(qt, kp, vt, km)


def _gla_kernel(q_ref, k_ref, v_ref, r_ref, cum_ref, gain_ref, o_ref, state_ref, *, rows):
    @pl.when(pl.program_id(1) == 0)
    def _():
        state_ref[...] = jnp.zeros_like(state_ref)

    c = GLA_CHUNK
    r_i = lax.broadcasted_iota(jnp.int32, (c, c), 0)
    c_i = lax.broadcasted_iota(jnp.int32, (c, c), 1)
    lower = r_i >= c_i
    nt = (((1,), (1,)), ((), ()))
    tn = (((0,), (0,)), ((), ()))
    bf = jnp.bfloat16
    f32 = jnp.float32
    for j in range(rows // c):
        sl = slice(j * c, (j + 1) * c)
        for h in range(GLA_HEADS):
            hl = slice(h * LANES, (h + 1) * LANES)
            cum = cum_ref[sl, hl]
            cum_last = cum[c - 1:c, :]
            q = q_ref[sl, hl].astype(f32) * (GLA_KEY_DIM ** -0.5)
            k = k_ref[sl, hl].astype(f32)
            v = v_ref[sl, hl]
            q_dec = (q * jnp.exp(cum)).astype(bf)
            k_inv = (k * jnp.exp(-cum)).astype(bf)
            k_end = (k * jnp.exp(cum_last - cum)).astype(bf)
            att = lax.dot_general(q_dec, k_inv, nt, preferred_element_type=f32)
            att = jnp.where(lower, att, 0.0)
            intra = jnp.dot(att.astype(bf), v, preferred_element_type=f32)
            state = state_ref[h]
            inter = lax.dot_general(q_dec, state.astype(bf), nt, preferred_element_type=f32)
            d_state = lax.dot_general(v, k_end, tn, preferred_element_type=f32)
            state_ref[h] = state * jnp.exp(cum_last) + d_state
            o = intra + inter
            mu = jnp.mean(o, axis=-1, keepdims=True)
            var = jnp.mean(jnp.square(o - mu), axis=-1, keepdims=True)
            on = (o - mu) * lax.rsqrt(var + LN_EPS) * gain_ref[:, hl]
            rg = r_ref[sl, hl].astype(f32)
            o_ref[sl, hl] = (on * (rg * jax.nn.sigmoid(rg))).astype(o_ref.dtype)


def gla(proj, cum, gain, batch, seq, rows):
    t = proj.shape[0]
    assert seq % rows == 0 and rows % GLA_CHUNK == 0
    ns = seq // rows
    col = lambda c0: pl.BlockSpec((rows, GLA_WIDTH), lambda b, i: (b * ns + i, c0 // GLA_WIDTH))
    return pl.pallas_call(
        functools.partial(_gla_kernel, rows=rows),
        grid=(batch, ns),
        in_specs=[col(COL_QG), col(COL_KG), col(COL_VG), col(COL_RG), col(0),
                  pl.BlockSpec((1, GLA_WIDTH), lambda b, i: (0, 0))],
        out_specs=col(0),
        out_shape=jax.ShapeDtypeStruct((t, GLA_WIDTH), jnp.bfloat16),
        scratch_shapes=[pltpu.VMEM((GLA_HEADS, LANES, LANES), jnp.float32)],
        compiler_params=_params(("parallel", "arbitrary")),
        name="gla",
    )(proj, proj, proj, proj, cum, gain)


def _layer_norm(z, g, b):
    mu = jnp.mean(z, axis=-1, keepdims=True)
    var = jnp.mean(jnp.square(z - mu), axis=-1, keepdims=True)
    return (z - mu) * lax.rsqrt(var + LN_EPS) * g + b


def _merge_kernel(ya_ref, yb_ref, ga_ref, gb_ref, x_ref, wa_ref, wb_ref, wo_ref, g_ref, b_ref, xo_ref, xb_ref):
    bf = jnp.bfloat16
    f32 = jnp.float32
    pa = jnp.dot(ya_ref[...], wa_ref[...], preferred_element_type=f32)
    pb = jnp.dot(yb_ref[...], wb_ref[...], preferred_element_type=f32)
    merged = jax.nn.sigmoid(ga_ref[...].astype(f32)) * pa + jax.nn.sigmoid(gb_ref[...].astype(f32)) * pb
    mix = jnp.dot(merged.astype(bf), wo_ref[...], preferred_element_type=jnp.float32)
    y = _layer_norm(DEEPNORM_ALPHA * x_ref[...] + mix, g_ref[...], b_ref[...])
    xo_ref[...] = y
    xb_ref[...] = y.astype(bf)


def merge_out(ya, yb, proj, x, wa, wb, wo, g, b, tm):
    t = x.shape[0]
    row = lambda w, cb=0: pl.BlockSpec((tm, w), lambda i: (i, cb))
    const = lambda r, c: pl.BlockSpec((r, c), lambda i: (0, 0))
    return pl.pallas_call(
        _merge_kernel,
        grid=(t // tm,),
        in_specs=[row(MOBA_WIDTH), row(GLA_WIDTH), row(D_MODEL, COL_GATE_A // D_MODEL), row(D_MODEL, COL_GATE_B // D_MODEL),
                  row(D_MODEL), const(MOBA_WIDTH, D_MODEL), const(GLA_WIDTH, D_MODEL), const(D_MODEL, D_MODEL),
                  const(1, D_MODEL), const(1, D_MODEL)],
        out_specs=[row(D_MODEL), row(D_MODEL)],
        out_shape=[jax.ShapeDtypeStruct((t, D_MODEL), jnp.float32), jax.ShapeDtypeStruct((t, D_MODEL), jnp.bfloat16)],
        compiler_params=_params(("parallel",)),
        name="merge_out_ln",
    )(ya, yb, proj, proj, x, wa, wb, wo, g, b)


def _ffn_kernel(xb_ref, x_ref, wg_ref, wu_ref, wd_ref, g_ref, b_ref, xo_ref, xbo_ref, acc_ref):
    f = pl.program_id(1)

    @pl.when(f == 0)
    def _():
        acc_ref[...] = jnp.zeros_like(acc_ref)

    h = xb_ref[...]
    gate = jnp.dot(h, wg_ref[...], preferred_element_type=jnp.float32)
    up = jnp.dot(h, wu_ref[...], preferred_element_type=jnp.float32)
    hid = gate * jax.nn.sigmoid(gate) * up
    acc_ref[...] += jnp.dot(hid.astype(jnp.bfloat16), wd_ref[...], preferred_element_type=jnp.float32)

    @pl.when(f == pl.num_programs(1) - 1)
    def _():
        y = _layer_norm(DEEPNORM_ALPHA * x_ref[...] + acc_ref[...], g_ref[...], b_ref[...])
        xo_ref[...] = y
        xbo_ref[...] = y.astype(jnp.bfloat16)


def ffn(xb, x, wg, wu, wd, g, b, tm, tf):
    t = x.shape[0]
    fdim = wg.shape[1]
    row = lambda w: pl.BlockSpec((tm, w), lambda i, f: (i, 0))
    const = pl.BlockSpec((1, D_MODEL), lambda i, f: (0, 0))
    return pl.pallas_call(
        _ffn_kernel,
        grid=(t // tm, fdim // tf),
        in_specs=[row(D_MODEL), row(D_MODEL),
                  pl.BlockSpec((D_MODEL, tf), lambda i, f: (0, f)),
                  pl.BlockSpec((D_MODEL, tf), lambda i, f: (0, f)),
                  pl.BlockSpec((tf, D_MODEL), lambda i, f: (f, 0)),
                  const, const],
        out_specs=[row(D_MODEL), row(D_MODEL)],
        out_shape=[jax.ShapeDtypeStruct((t, D_MODEL), jnp.float32), jax.ShapeDtypeStruct((t, D_MODEL), jnp.bfloat16)],
        scratch_shapes=[pltpu.VMEM((tm, D_MODEL), jnp.float32)],
        compiler_params=_params(("parallel", "arbitrary")),
        name="swiglu_ln",
    )(xb, x, wg, wu, wd, g, b)


ROUTE_E1, ROUTE_E2, ROUTE_W1, ROUTE_W2 = 0, 1, 2, 3


def _router_kernel(x_ref, w_ref, o_ref):
    x_hi, x_mid, _ = _split_bf16(x_ref[...])
    w_hi, w_mid, _ = _split_bf16(w_ref[...])
    dot = functools.partial(jnp.dot, preferred_element_type=jnp.float32)
    logits = dot(x_hi, w_hi) + (dot(x_hi, w_mid) + dot(x_mid, w_hi))
    lane = lax.broadcasted_iota(jnp.int32, (1, LANES), 1)
    g = jnp.where(lane < N_EXPERTS, logits, -jnp.inf)
    m1 = jnp.max(g, axis=-1, keepdims=True)
    i1 = jnp.min(jnp.where(g == m1, lane, LANES), axis=-1, keepdims=True)
    g2 = jnp.where(lane == i1, -jnp.inf, g)
    m2 = jnp.max(g2, axis=-1, keepdims=True)
    i2 = jnp.min(jnp.where(g2 == m2, lane, LANES), axis=-1, keepdims=True)
    e2 = jnp.exp(m2 - m1)
    w1 = 1.0 / (1.0 + e2)
    w2 = e2 / (1.0 + e2)
    out = jnp.where(lane == ROUTE_E1, i1.astype(jnp.float32), 0.0)
    out = jnp.where(lane == ROUTE_E2, i2.astype(jnp.float32), out)
    out = jnp.where(lane == ROUTE_W1, w1, out)
    o_ref[...] = jnp.where(lane == ROUTE_W2, w2, out)


def router(x, w, tm):
    t = x.shape[0]
    return pl.pallas_call(
        _router_kernel,
        grid=(t // tm,),
        in_specs=[pl.BlockSpec((tm, D_MODEL), lambda i: (i, 0)), pl.BlockSpec((D_MODEL, LANES), lambda i: (0, 0))],
        out_specs=pl.BlockSpec((tm, LANES), lambda i: (i, 0)),
        out_shape=jax.ShapeDtypeStruct((t, LANES), jnp.float32),
        compiler_params=_params(("parallel",)),
        name="router",
    )(x, w)


def _routing_tables(route, tm):
    t = route.shape[0]
    experts = route[:, ROUTE_E1:ROUTE_E2 + 1].astype(jnp.int32).reshape(-1)
    onehot = (experts[:, None] == jnp.arange(N_EXPERTS, dtype=jnp.int32)[None, :]).astype(jnp.int32)
    csum = jnp.cumsum(onehot, axis=0)
    rank = jnp.sum((csum - onehot) * onehot, axis=1)
    size = -(-csum[-1] // tm) * tm
    end = jnp.cumsum(size)
    pos = jnp.sum(onehot * (end - size)[None, :], axis=1) + rank
    n_tiles = 2 * t // tm + N_EXPERTS
    tok = jnp.zeros(((n_tiles + 1) * tm,), jnp.int32).at[pos].set(
        jnp.arange(2 * t, dtype=jnp.int32) // 2, unique_indices=True, mode="promise_in_bounds")
    tile_start = jnp.arange(n_tiles, dtype=jnp.int32) * tm
    tile_expert = jnp.minimum(jnp.sum((tile_start[:, None] >= end[None, :]).astype(jnp.int32), axis=1), N_EXPERTS - 1)
    n_used = end[-1] // tm
    return tok.reshape(n_tiles + 1, tm), pos.reshape(t, 2), tile_expert, n_used.reshape(1)


SUBLANES = 8


def _issue_row_copies(slot, n_rows, idx_ref, src_hbm, dst_ref, sem):
    for s in range(2):
        @pl.when(slot == s)
        def _():
            def body(j, c):
                base = pl.multiple_of(j * SUBLANES, SUBLANES)
                for u in range(SUBLANES):
                    dst = (dst_ref.at[s, j, pl.ds(u, 1), :] if len(dst_ref.shape) == 4
                           else dst_ref.at[s, pl.ds(base + u, 1), :])
                    pltpu.make_async_copy(src_hbm.at[pl.ds(idx_ref[s, base + u], 1), :], dst, sem.at[s]).start()
                return c
            lax.fori_loop(0, n_rows // SUBLANES, body, 0)


def _moe_kernel(te_ref, nu_ref, tok_hbm, x_hbm, wg_ref, wu_ref, wd_ref, y_ref,
                idx_ref, xg_ref, xb_ref, acc_ref, idx_sem, row_sem, *, tm):
    i = pl.program_id(0)
    f = pl.program_id(1)
    nt = pl.num_programs(0)
    nf = pl.num_programs(1)
    slot = i % 2
    nxt = 1 - slot
    chunk = tm // (nf + 1)

    def idx_copy(tile, s):
        return pltpu.make_async_copy(tok_hbm.at[tile], idx_ref.at[s], idx_sem.at[s])

    def rows_done(s):
        return pltpu.make_async_copy(xg_ref.at[s], xg_ref.at[s], row_sem.at[s])

    def issue_chunk():
        first = (f + 1) * chunk
        for u in range(chunk):
            dst = xg_ref.at[nxt, first // SUBLANES + u // SUBLANES, pl.ds(u % SUBLANES, 1), :]
            pltpu.make_async_copy(x_hbm.at[pl.ds(idx_ref[nxt, first + u], 1), :], dst, row_sem.at[nxt]).start()

    @pl.when(f == 0)
    def _():
        @pl.when(i == 0)
        def _():
            idx_copy(0, 0).start()
            idx_copy(0, 0).wait()
            _issue_row_copies(0, tm, idx_ref, x_hbm, xg_ref, row_sem)
            idx_copy(1, 1).start()

        idx_copy(i + 1, nxt).wait()
        _issue_row_copies(nxt, chunk, idx_ref, x_hbm, xg_ref, row_sem)
        rows_done(slot).wait()
        xb_ref[...] = xg_ref[slot].reshape(tm, D_MODEL).astype(jnp.bfloat16)
        acc_ref[...] = jnp.zeros_like(acc_ref)

    @pl.when((f == 1) & (i + 1 < nt))
    def _():
        idx_copy(i + 2, slot).start()

    valid = i < nu_ref[0]

    @pl.when(valid)
    def _():
        issue_chunk()
        h = xb_ref[...]
        bf = jnp.bfloat16
        gate = jnp.dot(h, wg_ref[0].astype(bf), preferred_element_type=jnp.float32)
        up = jnp.dot(h, wu_ref[0].astype(bf), preferred_element_type=jnp.float32)
        hid = gate * jax.nn.sigmoid(gate) * up
        acc_ref[...] += jnp.dot(hid.astype(bf), wd_ref[0].astype(bf), preferred_element_type=jnp.float32)

    @pl.when(jnp.logical_not(valid))
    def _():
        issue_chunk()

    @pl.when(f == nf - 1)
    def _():
        y_ref[...] = acc_ref[...]

        @pl.when(i == nt - 1)
        def _():
            rows_done(nxt).wait()


def moe_experts(tile_expert, n_used, tok, x, wg, wu, wd, tm, tf):
    n_tiles = tok.shape[0] - 1
    fdim = wg.shape[2]
    assert fdim // tf >= 2 and n_tiles >= 2 and tm % (SUBLANES * (fdim // tf + 1)) == 0
    grid_spec = pltpu.PrefetchScalarGridSpec(
        num_scalar_prefetch=2,
        grid=(n_tiles, fdim // tf),
        in_specs=[pl.BlockSpec(memory_space=pl.ANY), pl.BlockSpec(memory_space=pl.ANY),
                  pl.BlockSpec((1, D_MODEL, tf), lambda i, f, te, nu: (te[i], 0, f)),
                  pl.BlockSpec((1, D_MODEL, tf), lambda i, f, te, nu: (te[i], 0, f)),
                  pl.BlockSpec((1, tf, D_MODEL), lambda i, f, te, nu: (te[i], f, 0))],
        out_specs=pl.BlockSpec((tm, D_MODEL), lambda i, f, te, nu: (i, 0)),
        scratch_shapes=[pltpu.SMEM((2, tm), jnp.int32), pltpu.VMEM((2, tm // SUBLANES, SUBLANES, D_MODEL), jnp.float32),
                        pltpu.VMEM((tm, D_MODEL), jnp.bfloat16), pltpu.VMEM((tm, D_MODEL), jnp.float32),
                        pltpu.SemaphoreType.DMA((2,)), pltpu.SemaphoreType.DMA((2,))])
    return pl.pallas_call(
        functools.partial(_moe_kernel, tm=tm),
        grid_spec=grid_spec,
        out_shape=jax.ShapeDtypeStruct((n_tiles * tm, D_MODEL), jnp.float32),
        compiler_params=_params(("arbitrary", "arbitrary"), disable_bounds_checks=True),
        name="moe_experts",
    )(tile_expert, n_used, tok, x, wg, wu, wd)


def _combine_kernel(pos_hbm, y_hbm, route_ref, x_ref, g_ref, b_ref, xo_ref, idx_ref, rows_ref, idx_sem, row_sem, *, tc):
    i = pl.program_id(0)
    nt = pl.num_programs(0)
    slot = i % 2

    def idx_copy(tile, s):
        return pltpu.make_async_copy(pos_hbm.at[tile], idx_ref.at[s], idx_sem.at[s])

    def gather_rows(s):
        _issue_row_copies(s, 2 * tc, idx_ref, y_hbm, rows_ref, row_sem)

    @pl.when(i == 0)
    def _():
        idx_copy(0, 0).start()
        idx_copy(0, 0).wait()
        gather_rows(0)
        idx_copy(1, 1).start()

    @pl.when(i + 1 < nt)
    def _():
        idx_copy(i + 1, 1 - slot).wait()
        gather_rows(1 - slot)

    pltpu.make_async_copy(rows_ref.at[slot], rows_ref.at[slot], row_sem.at[slot]).wait()

    @pl.when(i + 2 < nt)
    def _():
        idx_copy(i + 2, slot).start()

    route = route_ref[...]
    w1 = route[:, ROUTE_W1:ROUTE_W1 + 1]
    w2 = route[:, ROUTE_W2:ROUTE_W2 + 1]
    groups = tc // SUBLANES
    f = w1 * rows_ref[slot, :groups].reshape(tc, D_MODEL) + w2 * rows_ref[slot, groups:].reshape(tc, D_MODEL)
    xo_ref[...] = _layer_norm(DEEPNORM_ALPHA * x_ref[...] + f, g_ref[...], b_ref[...])


def moe_combine(pos, y, route, x, g, b, tc):
    t = x.shape[0]
    n_tiles = t // tc
    assert n_tiles >= 2
    pos_tiles = pos.reshape(n_tiles, tc, 2).transpose(0, 2, 1).reshape(n_tiles, 2 * tc)
    const = pl.BlockSpec((1, D_MODEL), lambda i: (0, 0))
    return pl.pallas_call(
        functools.partial(_combine_kernel, tc=tc),
        grid=(n_tiles,),
        in_specs=[pl.BlockSpec(memory_space=pl.ANY), pl.BlockSpec(memory_space=pl.ANY),
                  pl.BlockSpec((tc, LANES), lambda i: (i, 0)), pl.BlockSpec((tc, D_MODEL), lambda i: (i, 0)),
                  const, const],
        out_specs=pl.BlockSpec((tc, D_MODEL), lambda i: (i, 0)),
        out_shape=jax.ShapeDtypeStruct((t, D_MODEL), jnp.float32),
        scratch_shapes=[pltpu.SMEM((2, 2 * tc), jnp.int32),
                        pltpu.VMEM((2, 2 * tc // SUBLANES, SUBLANES, D_MODEL), jnp.float32),
                        pltpu.SemaphoreType.DMA((2,)), pltpu.SemaphoreType.DMA((2,))],
        compiler_params=_params(("arbitrary",), disable_bounds_checks=True),
        name="moe_combine_ln",
    )(pos_tiles, y, route, x, g, b)


def _rope_tables(seq):
    half = ROPE_DIM // 2
    inv_freq = jnp.power(ROPE_THETA, -jnp.arange(0, ROPE_DIM, 2, dtype=jnp.float32) / ROPE_DIM)
    ang = jnp.arange(seq, dtype=jnp.float32)[:, None] * inv_freq[None, :]
    cos, sin = jnp.cos(ang), jnp.sin(ang)
    ones = jnp.ones((seq, MOBA_HEAD_DIM - ROPE_DIM), jnp.float32)
    zeros = jnp.zeros((seq, MOBA_HEAD_DIM - ROPE_DIM), jnp.float32)
    z8 = jnp.zeros((seq, half), jnp.float32)
    tab_c = jnp.concatenate([cos, cos, ones], axis=1)
    tab_a = jnp.concatenate([-sin, z8, zeros], axis=1)
    tab_b = jnp.concatenate([z8, sin, zeros], axis=1)
    dup = lambda t: jnp.concatenate([t, t], axis=1)
    return dup(tab_c), dup(tab_a), dup(tab_b)


def _split_w_in(w):
    offs = np.cumsum([0, 512, 512, 512, 512, 512, 512, 512, GLA_GATE_RANK, 1024, 1024])
    part = lambda j: w[:, int(offs[j]):int(offs[j + 1])]
    qa, ka, va, qg, kg, vg, rg, lg, gate_a, gate_b = [part(j) for j in range(10)]
    main = jnp.concatenate([gate_a, gate_b, qa, ka, va, qg, kg, vg, rg], axis=1).astype(jnp.bfloat16)
    lg = jnp.pad(lg, ((0, 0), (0, LANES - GLA_GATE_RANK))).astype(jnp.bfloat16)
    return main, lg


def kernel(x, w_in, w_gla_gate_up, b_gla_gate, gla_norm_g, w_branch_a, w_branch_b, w_out, ln_mix_g, ln_mix_b,
           ffn_w_gate, ffn_w_up, ffn_w_down, moe_w_router, moe_w_gate, moe_w_up, moe_w_down, ln_ffn_g, ln_ffn_b):
    batch, seq, d = x.shape
    t = batch * seq
    bf16_rows = 16
    nb_pad = -(-(seq // MOBA_BLOCK) // bf16_rows) * bf16_rows
    assert d == D_MODEL and seq % MOBA_BLOCK == 0 and nb_pad <= LANES - MOBA_HEAD_DIM
    bf = jnp.bfloat16
    tab_c, tab_a, tab_b = _rope_tables(seq)
    xf = x.reshape(t, d)
    xb = xf
    row_tile = math.gcd(t, 1024)
    for layer in range(DEPTH):
        w_main, w_lg = _split_w_in(w_in[layer])
        w_up = jnp.pad(w_gla_gate_up[layer], ((0, LANES - GLA_GATE_RANK), (0, 0)))
        proj, cum = in_proj(xb, w_main, w_lg, w_up, b_gla_gate[layer].reshape(1, -1), math.gcd(t, 2048), 512)
        qt, kp, vt, kmean = rope_prepass(proj, tab_c, tab_a, tab_b, batch, seq)
        km = jnp.pad(kmean.transpose(0, 2, 1, 3), ((0, 0), (0, 0), (0, nb_pad - seq // MOBA_BLOCK), (0, 0)))
        y_a = moba_attention(qt, kp, vt, km, batch, seq)
        y_b = gla(proj, cum, gla_norm_g[layer].reshape(1, -1), batch, seq, math.gcd(seq, 1024))
        xf, xb = merge_out(y_a, y_b, proj, xf, w_branch_a[layer].astype(bf), w_branch_b[layer].astype(bf),
                           w_out[layer].astype(bf), ln_mix_g[layer].reshape(1, -1), ln_mix_b[layer].reshape(1, -1),
                           math.gcd(t, 512))
        j = layer // 2
        ln_g, ln_b = ln_ffn_g[layer].reshape(1, -1), ln_ffn_b[layer].reshape(1, -1)
        if layer % 2 == 0:
            xf, xb = ffn(xb, xf, ffn_w_gate[j].astype(bf), ffn_w_up[j].astype(bf), ffn_w_down[j].astype(bf),
                         ln_g, ln_b, row_tile, 512)
        else:
            w_r = jnp.pad(moe_w_router[j], ((0, 0), (0, LANES - N_EXPERTS)))
            route = router(xf, w_r, math.gcd(t, 512))
            tok, pos, tile_expert, n_used = _routing_tables(route, row_tile)
            y = moe_experts(tile_expert, n_used, tok, xf, moe_w_gate[j], moe_w_up[j], moe_w_down[j], row_tile, 512)
            xf = moe_combine(pos, y, route, xf, ln_g, ln_b, math.gcd(t, 512))
    return xf.reshape(batch, seq, d)
```
